```python
import math
import jax, jax.numpy as jnp
from jax import lax
import numpy as np

D_MODEL = 1024
BATCH = 8
SEQ = 4096
DEPTH = 1

LRU_WIDTH = 1024
LRU_BLOCKS = 8
LRU_BLOCK = LRU_WIDTH // LRU_BLOCKS
CONV_WIDTH = 4
LRU_C = 8.0
ATT_HEADS = 8
HEAD_DIM = 64
V_DIM = 2 * HEAD_DIM
ATT_WIDTH = ATT_HEADS * V_DIM
QK_WIDTH = ATT_HEADS * 2 * HEAD_DIM
D_MIX = LRU_WIDTH + ATT_WIDTH
OFF_LRU_X = 0
OFF_LRU_G = OFF_LRU_X + LRU_WIDTH
OFF_Q = OFF_LRU_G + LRU_WIDTH
OFF_K = OFF_Q + QK_WIDTH
OFF_V = OFF_K + QK_WIDTH
OFF_ATT_G = OFF_V + ATT_WIDTH
D_IN = OFF_ATT_G + ATT_WIDTH
N_BUCKETS = 32
MAX_DISTANCE = 128
Q_BLOCK = 128
EPS = 1e-6
NEG_INF = -1e30

kernel_name = "hymba_rglru_diffattn_hybrid"


def rms_norm(x, g):
    xf = x.astype(jnp.float32)
    y = xf * lax.rsqrt(jnp.mean(xf * xf, axis=-1, keepdims=True) + EPS)
    return (y * g.astype(jnp.float32)).astype(x.dtype)


def t5_causal_buckets(rel):
    n = jnp.maximum(rel, 0)
    max_exact = N_BUCKETS // 2
    nf = jnp.maximum(n, 1).astype(jnp.float32)
    large = max_exact + (jnp.log(nf / max_exact) / math.log(MAX_DISTANCE / max_exact)
                         * (N_BUCKETS - max_exact)).astype(jnp.int32)
    large = jnp.minimum(large, N_BUCKETS - 1)
    return jnp.where(n < max_exact, n, large)


def causal_depthwise_conv(x, w, b):
    c = x.shape[-1]
    y = lax.conv_general_dilated(
        x, w[:, None, :].astype(x.dtype), window_strides=(1,),
        padding=[(CONV_WIDTH - 1, 0)], dimension_numbers=("NWC", "WIO", "NWC"),
        feature_group_count=c)
    return y + b.astype(x.dtype)


def rg_lru(x, w_r, b_r, w_i, b_i, lam):
    bsz, s, _ = x.shape
    xf = x.astype(jnp.float32)
    xh = xf.reshape(bsz, s, LRU_BLOCKS, LRU_BLOCK)
    r = jax.nn.sigmoid(jnp.einsum("bshi,hij->bshj", xh, w_r.astype(jnp.float32))
                       + b_r.astype(jnp.float32)).reshape(bsz, s, LRU_WIDTH)
    ig = jax.nn.sigmoid(jnp.einsum("bshi,hij->bshj", xh, w_i.astype(jnp.float32))
                        + b_i.astype(jnp.float32)).reshape(bsz, s, LRU_WIDTH)
    log_a = -LRU_C * r * jax.nn.softplus(-lam.astype(jnp.float32))
    a = jnp.exp(log_a)
    u = jnp.sqrt(-jnp.expm1(2.0 * log_a)) * (ig * xf)

    def combine(left, right):
        a_l, b_l = left
        a_r, b_r2 = right
        return a_l * a_r, a_r * b_l + b_r2

    _, h = lax.associative_scan(combine, (a, u), axis=1)
    return h.astype(x.dtype)


def diff_attention(q, k, v, rel_bias, lam):
    bsz, s = q.shape[0], q.shape[1]
    nb = s // Q_BLOCK
    scale = 1.0 / math.sqrt(HEAD_DIM)
    q1 = q[:, :, :, 0].transpose(0, 2, 1, 3)
    q2 = q[:, :, :, 1].transpose(0, 2, 1, 3)
    k1 = k[:, :, :, 0].transpose(0, 2, 1, 3)
    k2 = k[:, :, :, 1].transpose(0, 2, 1, 3)
    vt = v.transpose(0, 2, 1, 3)
    q1b = q1.reshape(bsz, ATT_HEADS, nb, Q_BLOCK, HEAD_DIM).transpose(2, 0, 1, 3, 4)
    q2b = q2.reshape(bsz, ATT_HEADS, nb, Q_BLOCK, HEAD_DIM).transpose(2, 0, 1, 3, 4)
    kpos = jnp.arange(s, dtype=jnp.int32)
    bias_tab = rel_bias.astype(jnp.float32)

    def block_fn(args):
        q1i, q2i, bi = args
        qpos = bi * Q_BLOCK + jnp.arange(Q_BLOCK, dtype=jnp.int32)
        rel = qpos[:, None] - kpos[None, :]
        bias = bias_tab[t5_causal_buckets(rel)].transpose(2, 0, 1)[None]
        mask = (rel >= 0)[None, None]
        s1 = jnp.einsum("bhqd,bhkd->bhqk", q1i, k1).astype(jnp.float32) * scale + bias
        s2 = jnp.einsum("bhqd,bhkd->bhqk", q2i, k2).astype(jnp.float32) * scale + bias
        p1 = jax.nn.softmax(jnp.where(mask, s1, NEG_INF), axis=-1)
        p2 = jax.nn.softmax(jnp.where(mask, s2, NEG_INF), axis=-1)
        p = (p1 - lam * p2).astype(vt.dtype)
        return jnp.einsum("bhqk,bhkv->bhqv", p, vt)

    out = lax.map(block_fn, (q1b, q2b, jnp.arange(nb, dtype=jnp.int32)))
    return out.transpose(1, 0, 3, 2, 4).reshape(bsz, s, ATT_HEADS, V_DIM)


def setup_inputs(seed: int = 0) -> dict:
    key = jax.random.key(seed)
    ks = jax.random.split(key, 20)
    f32 = jnp.float32
    nrm = lambda k, shp, sc: (jax.random.normal(k, shp, f32) * sc)
    x = jax.random.normal(ks[0], (BATCH, SEQ, D_MODEL), f32)
    norm_gain = 1.0 + nrm(ks[1], (DEPTH, D_MODEL), 0.02)
    w_in = nrm(ks[2], (DEPTH, D_MODEL, D_IN), D_MODEL ** -0.5)
    conv_w = nrm(ks[3], (DEPTH, CONV_WIDTH, LRU_WIDTH), CONV_WIDTH ** -0.5)
    conv_b = nrm(ks[4], (DEPTH, LRU_WIDTH), 0.02)
    w_rg = nrm(ks[5], (DEPTH, LRU_BLOCKS, LRU_BLOCK, LRU_BLOCK), LRU_BLOCK ** -0.5)
    b_rg = nrm(ks[6], (DEPTH, LRU_BLOCKS, LRU_BLOCK), 0.02)
    w_ig = nrm(ks[7], (DEPTH, LRU_BLOCKS, LRU_BLOCK, LRU_BLOCK), LRU_BLOCK ** -0.5)
    b_ig = nrm(ks[8], (DEPTH, LRU_BLOCKS, LRU_BLOCK), 0.02)
    a_c = jax.random.uniform(ks[9], (DEPTH, LRU_WIDTH), f32, 0.9, 0.999)
    a = a_c ** (1.0 / LRU_C)
    lru_lambda = jnp.log(a) - jnp.log1p(-a)
    q_norm_gain = 1.0 + nrm(ks[10], (DEPTH, HEAD_DIM), 0.02)
    k_norm_gain = 1.0 + nrm(ks[11], (DEPTH, HEAD_DIM), 0.02)
    lambda_q1 = nrm(ks[12], (DEPTH, HEAD_DIM), 0.1)
    lambda_k1 = nrm(ks[13], (DEPTH, HEAD_DIM), 0.1)
    lambda_q2 = nrm(ks[14], (DEPTH, HEAD_DIM), 0.1)
    lambda_k2 = nrm(ks[15], (DEPTH, HEAD_DIM), 0.1)
    subln_gain = 1.0 + nrm(ks[16], (DEPTH, V_DIM), 0.02)
    w_out = nrm(ks[17], (DEPTH, D_MIX, D_MODEL), D_MIX ** -0.5)
    rel_bias = nrm(ks[18], (N_BUCKETS, ATT_HEADS), 0.5)
    return {"x": x, "norm_gain": norm_gain, "w_in": w_in, "conv_w": conv_w,
            "conv_b": conv_b, "w_rg": w_rg, "b_rg": b_rg, "w_ig": w_ig, "b_ig": b_ig,
            "lru_lambda": lru_lambda, "q_norm_gain": q_norm_gain,
            "k_norm_gain": k_norm_gain, "lambda_q1": lambda_q1, "lambda_k1": lambda_k1,
            "lambda_q2": lambda_q2, "lambda_k2": lambda_k2, "subln_gain": subln_gain,
            "w_out": w_out, "rel_bias": rel_bias}


def reference(x, norm_gain, w_in, conv_w, conv_b, w_rg, b_rg, w_ig, b_ig, lru_lambda,
              q_norm_gain, k_norm_gain, lambda_q1, lambda_k1, lambda_q2, lambda_k2,
              subln_gain, w_out, rel_bias):
    bsz, s, _ = x.shape
    for l in range(DEPTH):
        h = rms_norm(x, norm_gain[l])
        z = jnp.einsum("bsd,de->bse", h, w_in[l])
        x_lru = z[..., OFF_LRU_X:OFF_LRU_G]
        g_lru = z[..., OFF_LRU_G:OFF_Q]
        zq = z[..., OFF_Q:OFF_K].reshape(bsz, s, ATT_HEADS, 2, HEAD_DIM)
        zk = z[..., OFF_K:OFF_V].reshape(bsz, s, ATT_HEADS, 2, HEAD_DIM)
        zv = z[..., OFF_V:OFF_ATT_G].reshape(bsz, s, ATT_HEADS, V_DIM)
        g_att = z[..., OFF_ATT_G:D_IN]

        xc = causal_depthwise_conv(x_lru, conv_w[l], conv_b[l])
        y_lru = rg_lru(xc, w_rg[l], b_rg[l], w_ig[l], b_ig[l], lru_lambda[l])
        y_lru = y_lru * jax.nn.silu(g_lru)

        q = rms_norm(zq, q_norm_gain[l])
        k = rms_norm(zk, k_norm_gain[l])
        lam_init = 0.8 - 0.6 * math.exp(-0.3 * l)
        lam = (jnp.exp(jnp.sum(lambda_q1[l].astype(jnp.float32) * lambda_k1[l].astype(jnp.float32)))
               - jnp.exp(jnp.sum(lambda_q2[l].astype(jnp.float32) * lambda_k2[l].astype(jnp.float32)))
               + lam_init)
        att = diff_attention(q, k, zv, rel_bias, lam)
        att = rms_norm(att, subln_gain[l]) * (1.0 - lam_init)
        y_att = att.reshape(bsz, s, ATT_WIDTH) * jax.nn.silu(g_att)

        mix = jnp.concatenate([y_lru, y_att], axis=-1)
        x = x + jnp.einsum("bse,ed->bsd", mix, w_out[l])
    return x
```

```python
import functools
import math

import jax
import jax.numpy as jnp
import numpy as np
from jax import lax
from jax.experimental import pallas as pl
from jax.experimental.pallas import tpu as pltpu

F32 = jnp.float32
BF16 = jnp.bfloat16

D_MODEL = 1024
LRU_WIDTH = 1024
LRU_BLOCKS = 8
LRU_BLOCK = LRU_WIDTH // LRU_BLOCKS
CONV_WIDTH = 4
LRU_C = 8.0
ATT_HEADS = 8
HEAD_DIM = 64
V_DIM = 2 * HEAD_DIM
SEC = 1024
N_BUCKETS = 32
MAX_DISTANCE = 128
EPS = 1e-6
NEG_INF = -1e30
LAM_INIT = 0.8 - 0.6 * math.exp(-0.3 * 0)

SUBLANES = 8
TM = 512
TA = 256
SCAN_ROWS = 32
VMEM_LIMIT = 56 * 1024 * 1024


def _sigmoid(x):
    return 1.0 / (1.0 + jnp.exp(-x))


def _inproj_kernel(x_ref, ng_ref, w_ref, cw_ref, cb_ref, wg_ref, bg_ref, lam_ref,
                   gq_ref, gk_ref, gsum_ref,
                   ylru_ref, q_ref, k_ref, v_ref, sg_ref,
                   xbuf, a_buf, u_buf, hcar):
    t = pl.program_id(1)

    @pl.when(t == 0)
    def _():
        xbuf[0:SUBLANES, :] = jnp.zeros((SUBLANES, LRU_WIDTH), F32)
        hcar[...] = jnp.zeros_like(hcar)

    x = x_ref[0]
    ms = jnp.mean(x * x, axis=-1, keepdims=True)
    hb = (x * lax.rsqrt(ms + EPS) * ng_ref[...]).astype(BF16)

    def proj(c):
        return jnp.dot(hb, w_ref[:, c * SEC:(c + 1) * SEC], preferred_element_type=F32)

    xl = proj(0)
    xbuf[SUBLANES:SUBLANES + TM, :] = xl
    xc = cw_ref[3:4, :] * xl + cb_ref[...]
    for d in range(1, CONV_WIDTH):
        xc = xc + cw_ref[3 - d:4 - d, :] * xbuf[SUBLANES - d:SUBLANES - d + TM, :]
    xbuf[0:SUBLANES, :] = xbuf[TM:TM + SUBLANES, :]

    z = -lam_ref[...]
    neg_c_softplus = -LRU_C * (jnp.maximum(z, 0.0) + jnp.log1p(jnp.exp(-jnp.abs(z))))
    xcb = xc.astype(BF16)
    for blk in range(LRU_BLOCKS):
        cs = slice(blk * LRU_BLOCK, (blk + 1) * LRU_BLOCK)
        g = jnp.dot(xcb[:, cs], wg_ref[blk], preferred_element_type=F32) + bg_ref[blk]
        r = _sigmoid(g[:, :LRU_BLOCK])
        ig = _sigmoid(g[:, LRU_BLOCK:])
        a = jnp.exp(neg_c_softplus[:, cs] * r)
        a_buf[:, cs] = a
        u_buf[:, cs] = jnp.sqrt(1.0 - a * a) * (ig * xc[:, cs])

    row = lax.broadcasted_iota(jnp.int32, (SUBLANES, LRU_WIDTH), 0)

    def scan_body(g, carry):
        for s in range(SCAN_ROWS // SUBLANES):
            r0 = pl.multiple_of(g * SCAN_ROWS + s * SUBLANES, SUBLANES)
            a = a_buf[pl.ds(r0, SUBLANES), :]
            u = u_buf[pl.ds(r0, SUBLANES), :]
            for k in (1, 2, 4):
                keep = row >= k
                a_sh = jnp.where(keep, pltpu.roll(a, k, 0), 1.0)
                u_sh = jnp.where(keep, pltpu.roll(u, k, 0), 0.0)
                u = a * u_sh + u
                a = a * a_sh
            h = a * carry + u
            u_buf[pl.ds(r0, SUBLANES), :] = h
            carry = h[SUBLANES - 1:SUBLANES, :]
        return carry

    hcar[...] = lax.fori_loop(0, TM // SCAN_ROWS, scan_body, hcar[...])

    gl = proj(1)
    ylru_ref[0] = (u_buf[...] * (gl * _sigmoid(gl))).astype(BF16)

    def qk_norm(zc, gain):
        sq = (zc * zc).astype(BF16)
        parts = []
        for c in range(SEC // 256):
            parts.append(jnp.dot(sq[:, c * 256:(c + 1) * 256], gsum_ref[...],
                                 preferred_element_type=F32))
        ss = jnp.concatenate(parts, axis=-1)
        return (zc * lax.rsqrt(ss * (1.0 / HEAD_DIM) + EPS) * gain).astype(BF16)

    q_ref[0] = qk_norm(proj(2), gq_ref[...] * (1.0 / math.sqrt(HEAD_DIM)))
    k_ref[0] = qk_norm(proj(3), gk_ref[...])
    v_ref[0] = proj(4).astype(BF16)
    ga = proj(5)
    sg_ref[0] = (ga * _sigmoid(ga)).astype(BF16)


def _inproj(x, ng, w_in, cw, cb, wg, bg, lam, gq, gk, gsum):
    bsz, seq, _ = x.shape
    d_in = w_in.shape[1]
    const2 = lambda b, t: (0, 0)
    const3 = lambda b, t: (0, 0, 0)
    tile = pl.BlockSpec((1, TM, SEC), lambda b, t: (b, t, 0))
    out_sds = jax.ShapeDtypeStruct((bsz, seq, SEC), BF16)
    return pl.pallas_call(
        _inproj_kernel,
        grid=(bsz, seq // TM),
        in_specs=[
            pl.BlockSpec((1, TM, D_MODEL), lambda b, t: (b, t, 0)),
            pl.BlockSpec((1, D_MODEL), const2),
            pl.BlockSpec((D_MODEL, d_in), const2, pipeline_mode=pl.Buffered(1)),
            pl.BlockSpec((CONV_WIDTH, LRU_WIDTH), const2),
            pl.BlockSpec((1, LRU_WIDTH), const2),
            pl.BlockSpec((LRU_BLOCKS, LRU_BLOCK, 2 * LRU_BLOCK), const3),
            pl.BlockSpec((LRU_BLOCKS, 1, 2 * LRU_BLOCK), const3),
            pl.BlockSpec((1, LRU_WIDTH), const2),
            pl.BlockSpec((1, SEC), const2),
            pl.BlockSpec((1, SEC), const2),
            pl.BlockSpec((256, 256), const2),
        ],
        out_specs=[tile] * 5,
        out_shape=[out_sds] * 5,
        scratch_shapes=[
            pltpu.VMEM((TM + SUBLANES, LRU_WIDTH), F32),
            pltpu.VMEM((TM, LRU_WIDTH), F32),
            pltpu.VMEM((TM, LRU_WIDTH), F32),
            pltpu.VMEM((1, LRU_WIDTH), F32),
        ],
        compiler_params=pltpu.CompilerParams(
            dimension_semantics=("arbitrary", "arbitrary"),
            vmem_limit_bytes=VMEM_LIMIT),
        name="inproj_lru_qknorm",
    )(x, ng, w_in, cw, cb, wg, bg, lam, gq, gk, gsum)


def _bucket_table():
    rel = np.arange(TA)[:, None] - np.arange(2 * TA)[None, :] + TA
    n = np.maximum(rel, 0)
    max_exact = N_BUCKETS // 2
    nf = np.maximum(n, 1).astype(np.float32)
    large = max_exact + (np.log(nf / np.float32(max_exact)) / np.float32(math.log(MAX_DISTANCE / max_exact))
                         * np.float32(N_BUCKETS - max_exact)).astype(np.int32)
    large = np.minimum(large, N_BUCKETS - 1)
    bucket = np.where(n < max_exact, n, large)
    return np.where(rel >= 0, bucket, -1).astype(np.int32)


def _attn_kernel(rb_ref, bucket_ref, lq1_ref, lk1_ref, lq2_ref, lk2_ref, sgain_ref,
                 q_ref, k_ref, v_ref, sg_ref, o_ref,
                 w_ref, m_ref, l_ref, acc_ref):
    h = pl.program_id(0)
    b = pl.program_id(1)
    seq = q_ref.shape[1]

    @pl.when(b == 0)
    def _():
        bk = bucket_ref[...]
        far = rb_ref[N_BUCKETS - 1, h]
        w = jnp.full(bk.shape, NEG_INF, F32)
        for kk in range(N_BUCKETS):
            w = jnp.where(bk == kk, rb_ref[kk, h] - far, w)
        w_ref[...] = w

    lam = (jnp.exp(jnp.sum(lq1_ref[...] * lk1_ref[...], axis=-1, keepdims=True))
           - jnp.exp(jnp.sum(lq2_ref[...] * lk2_ref[...], axis=-1, keepdims=True))
           + LAM_INIT)
    lane = lax.broadcasted_iota(jnp.int32, (TA, V_DIM), 1)

    def tile_update(qq, k0, bias):
        kt = k_ref[0, pl.ds(k0, TA), :]
        vt = v_ref[0, pl.ds(k0, TA), :]
        s = lax.dot_general(qq, kt, (((1,), (1,)), ((), ())), preferred_element_type=F32)
        if bias is not None:
            s = s + jnp.concatenate([bias, bias], axis=0)
        m_prev = m_ref[...]
        m_new = jnp.maximum(m_prev, jnp.max(s, axis=-1, keepdims=True))
        alpha = jnp.exp(m_prev - m_new)
        p = jnp.exp(s - m_new)
        l_ref[...] = alpha * l_ref[...] + jnp.sum(p, axis=-1, keepdims=True)
        acc_ref[...] = alpha * acc_ref[...] + jnp.dot(p.astype(BF16), vt,
                                                       preferred_element_type=F32)
        m_ref[...] = m_new

    def q_body(i, carry):
        q0 = pl.multiple_of(i * TA, TA)
        q = q_ref[0, pl.ds(q0, TA), :].astype(F32)
        qq = jnp.concatenate([jnp.where(lane < HEAD_DIM, q, 0.0),
                              jnp.where(lane >= HEAD_DIM, q, 0.0)], axis=0).astype(BF16)
        m_ref[...] = jnp.full(m_ref.shape, NEG_INF, F32)
        l_ref[...] = jnp.zeros(l_ref.shape, F32)
        acc_ref[...] = jnp.zeros(acc_ref.shape, F32)

        def far_body(j, c):
            tile_update(qq, pl.multiple_of(j * TA, TA), None)
            return c

        lax.fori_loop(0, i - 1, far_body, 0)

        @pl.when(i > 0)
        def _():
            tile_update(qq, pl.multiple_of((i - 1) * TA, TA), w_ref[:, 0:TA])

        tile_update(qq, q0, w_ref[:, TA:2 * TA])

        inv_l = 1.0 / l_ref[...]
        o = acc_ref[...] * inv_l
        o = o[0:TA, :] - lam * o[TA:2 * TA, :]
        ms = jnp.mean(o * o, axis=-1, keepdims=True)
        y = o * lax.rsqrt(ms + EPS) * (sgain_ref[...] * (1.0 - LAM_INIT))
        y = y * sg_ref[0, pl.ds(q0, TA), :].astype(F32)
        o_ref[0, pl.ds(q0, TA), :] = y.astype(BF16)
        return carry

    lax.fori_loop(0, seq // TA, q_body, 0)


def _attention(rel_bias, lq1, lk1, lq2, lk2, sgain, q, k, v, sg):
    bsz, seq, _ = q.shape
    bucket = jnp.asarray(_bucket_table())
    const2 = lambda h, b: (0, 0)
    head = pl.BlockSpec((1, seq, V_DIM), lambda h, b: (b, 0, h))
    return pl.pallas_call(
        _attn_kernel,
        grid=(ATT_HEADS, bsz),
        in_specs=[
            pl.BlockSpec(memory_space=pltpu.SMEM),
            pl.BlockSpec((TA, 2 * TA), const2),
            pl.BlockSpec((1, HEAD_DIM), const2),
            pl.BlockSpec((1, HEAD_DIM), const2),
            pl.BlockSpec((1, HEAD_DIM), const2),
            pl.BlockSpec((1, HEAD_DIM), const2),
            pl.BlockSpec((1, V_DIM), const2),
            head, head, head, head,
        ],
        out_specs=head,
        out_shape=jax.ShapeDtypeStruct((bsz, seq, ATT_HEADS * V_DIM), BF16),
        scratch_shapes=[
            pltpu.VMEM((TA, 2 * TA), F32),
            pltpu.VMEM((2 * TA, 1), F32),
            pltpu.VMEM((2 * TA, 1), F32),
            pltpu.VMEM((2 * TA, V_DIM), F32),
        ],
        compiler_params=pltpu.CompilerParams(
            dimension_semantics=("arbitrary", "arbitrary"),
            vmem_limit_bytes=VMEM_LIMIT),
        name="diff_attention",
    )(rel_bias, bucket, lq1, lk1, lq2, lk2, sgain, q, k, v, sg)


def _outproj_kernel(x_ref, ya_ref, yb_ref, w_ref, o_ref):
    acc = jnp.dot(ya_ref[...], w_ref[0:LRU_WIDTH, :], preferred_element_type=F32)
    acc = acc + jnp.dot(yb_ref[...], w_ref[LRU_WIDTH:, :], preferred_element_type=F32)
    o_ref[...] = x_ref[...] + acc


def _outproj(x2, ya, yb, w_out):
    n, d = x2.shape
    row = lambda i: (i, 0)
    return pl.pallas_call(
        _outproj_kernel,
        grid=(n // TM,),
        in_specs=[
            pl.BlockSpec((TM, d), row),
            pl.BlockSpec((TM, ya.shape[1]), row),
            pl.BlockSpec((TM, yb.shape[1]), row),
            pl.BlockSpec(w_out.shape, lambda i: (0, 0)),
        ],
        out_specs=pl.BlockSpec((TM, d), row),
        out_shape=jax.ShapeDtypeStruct((n, d), F32),
        compiler_params=pltpu.CompilerParams(
            dimension_semantics=("arbitrary",), vmem_limit_bytes=VMEM_LIMIT),
        name="outproj_residual",
    )(x2, ya, yb, w_out)


def kernel(x, norm_gain, w_in, conv_w, conv_b, w_rg, b_rg, w_ig, b_ig, lru_lambda,
           q_norm_gain, k_norm_gain, lambda_q1, lambda_k1, lambda_q2, lambda_k2,
           subln_gain, w_out, rel_bias):
    bsz, seq, d = x.shape
    assert norm_gain.shape[0] == 1, "single layer only"
    assert seq % TM == 0 and seq % TA == 0 and TM % SCAN_ROWS == 0
    n_groups = SEC // HEAD_DIM
    wg = jnp.concatenate([w_rg[0], w_ig[0]], axis=-1).astype(BF16)
    bg = jnp.concatenate([b_rg[0], b_ig[0]], axis=-1)[:, None, :]
    gq = jnp.tile(q_norm_gain[0], n_groups)[None, :]
    gk = jnp.tile(k_norm_gain[0], n_groups)[None, :]
    gsum = jnp.asarray(np.kron(np.eye(256 // HEAD_DIM), np.ones((HEAD_DIM, HEAD_DIM))), BF16)

    ylru, q, k, v, sg = _inproj(
        x, norm_gain, w_in[0].astype(BF16), conv_w[0], conv_b, wg, bg, lru_lambda,
        gq, gk, gsum)
    yatt = _attention(rel_bias, lambda_q1, lambda_k1, lambda_q2, lambda_k2, subln_gain,
                      q, k, v, sg)
    out = _outproj(x.reshape(bsz * seq, d), ylru.reshape(bsz * seq, -1),
                   yatt.reshape(bsz * seq, -1), w_out[0].astype(BF16))
    return out.reshape(bsz, seq, d)
```

```python
import math

import jax
import jax.numpy as jnp
import numpy as np
from jax import lax
from jax.experimental import pallas as pl
from jax.experimental.pallas import tpu as pltpu

F32 = jnp.float32
BF16 = jnp.bfloat16

D_MODEL = 1024
LRU_WIDTH = 1024
LRU_BLOCKS = 8
LRU_BLOCK = LRU_WIDTH // LRU_BLOCKS
CONV_WIDTH = 4
LRU_C = 8.0
ATT_HEADS = 8
HEAD_DIM = 64
V_DIM = 2 * HEAD_DIM
SEC = 1024
N_BUCKETS = 32
MAX_DISTANCE = 128
EPS = 1e-6
NEG_INF = -1e30
LAM_INIT = 0.8 - 0.6 * math.exp(-0.3 * 0)
LOG2E = math.log2(math.e)

SUBLANES = 8
TM = 512
TA = 256
SCAN_ROWS = 32
SUM_ROWS = 16
VMEM_LIMIT = 58 * 1024 * 1024


def _sigmoid(x):
    return 1.0 / (1.0 + jnp.exp(-x))


def _inproj_kernel(x_ref, ng_ref, w_ref, cw_ref, cb_ref, wg_ref, bg_ref, lam_ref,
                   gq_ref, gk_ref, gsum_ref,
                   ylru_ref, qt_ref, k_ref, vt_ref, sg_ref,
                   xbuf, a_buf, u_buf, hcar):
    t = pl.program_id(1)

    @pl.when(t == 0)
    def _():
        xbuf[0:SUBLANES, :] = jnp.zeros((SUBLANES, LRU_WIDTH), F32)
        hcar[...] = jnp.zeros_like(hcar)

    x = x_ref[0]
    ms = jnp.mean(x * x, axis=-1, keepdims=True)
    hb = (x * lax.rsqrt(ms + EPS) * ng_ref[...]).astype(BF16)

    def proj(c):
        return jnp.dot(hb, w_ref[:, c * SEC:(c + 1) * SEC], preferred_element_type=F32)

    xl = proj(0)
    xbuf[SUBLANES:SUBLANES + TM, :] = xl
    xc = cw_ref[3:4, :] * xl + cb_ref[...]
    for d in range(1, CONV_WIDTH):
        xc = xc + cw_ref[3 - d:4 - d, :] * xbuf[SUBLANES - d:SUBLANES - d + TM, :]
    xbuf[0:SUBLANES, :] = xbuf[TM:TM + SUBLANES, :]

    z = -lam_ref[...]
    neg_c_softplus = -LRU_C * (jnp.maximum(z, 0.0) + jnp.log1p(jnp.exp(-jnp.abs(z))))
    xcb = xc.astype(BF16)
    for blk in range(LRU_BLOCKS):
        cs = slice(blk * LRU_BLOCK, (blk + 1) * LRU_BLOCK)
        g = jnp.dot(xcb[:, cs], wg_ref[blk], preferred_element_type=F32) + bg_ref[blk]
        r = _sigmoid(g[:, :LRU_BLOCK])
        ig = _sigmoid(g[:, LRU_BLOCK:])
        a = jnp.exp(neg_c_softplus[:, cs] * r)
        a_buf[:, cs] = a
        u_buf[:, cs] = jnp.sqrt(1.0 - a * a) * (ig * xc[:, cs])

    row = lax.broadcasted_iota(jnp.int32, (SUBLANES, LRU_WIDTH), 0)

    def scan_body(g, carry):
        for s in range(SCAN_ROWS // SUBLANES):
            r0 = pl.multiple_of(g * SCAN_ROWS + s * SUBLANES, SUBLANES)
            a = a_buf[pl.ds(r0, SUBLANES), :]
            u = u_buf[pl.ds(r0, SUBLANES), :]
            for k in (1, 2, 4):
                keep = row >= k
                a_sh = jnp.where(keep, pltpu.roll(a, k, 0), 1.0)
                u_sh = jnp.where(keep, pltpu.roll(u, k, 0), 0.0)
                u = a * u_sh + u
                a = a * a_sh
            h = a * carry + u
            u_buf[pl.ds(r0, SUBLANES), :] = h
            carry = h[SUBLANES - 1:SUBLANES, :]
        return carry

    hcar[...] = lax.fori_loop(0, TM // SCAN_ROWS, scan_body, hcar[...])

    gl = proj(1)
    ylru_ref[0] = (u_buf[...] * (gl * _sigmoid(gl))).astype(BF16)

    def qk_norm(zc, gain):
        sq = (zc * zc).astype(BF16)
        parts = []
        for c in range(SEC // 256):
            parts.append(jnp.dot(sq[:, c * 256:(c + 1) * 256], gsum_ref[...],
                                 preferred_element_type=F32))
        ss = jnp.concatenate(parts, axis=-1)
        return zc * lax.rsqrt(ss * (1.0 / HEAD_DIM) + EPS) * gain

    qn = qk_norm(proj(2), gq_ref[...] * (LOG2E / math.sqrt(HEAD_DIM)))
    qt_ref[0] = qn.T.astype(BF16)
    k_ref[0] = qk_norm(proj(3), gk_ref[...]).astype(BF16)
    a_buf[...] = proj(4)
    vt_ref[0] = a_buf[...].T.astype(BF16)
    ga = proj(5)
    sg_ref[0] = (ga * _sigmoid(ga)).astype(BF16)


def _inproj(x, ng, w_in, cw, cb, wg, bg, lam, gq, gk, gsum):
    bsz, seq, _ = x.shape
    d_in = w_in.shape[1]
    const2 = lambda b, t: (0, 0)
    const3 = lambda b, t: (0, 0, 0)
    tile = pl.BlockSpec((1, TM, SEC), lambda b, t: (b, t, 0))
    tile_t = pl.BlockSpec((1, SEC, TM), lambda b, t: (b, 0, t))
    sds = jax.ShapeDtypeStruct((bsz, seq, SEC), BF16)
    sds_t = jax.ShapeDtypeStruct((bsz, SEC, seq), BF16)
    return pl.pallas_call(
        _inproj_kernel,
        grid=(bsz, seq // TM),
        in_specs=[
            pl.BlockSpec((1, TM, D_MODEL), lambda b, t: (b, t, 0)),
            pl.BlockSpec((1, D_MODEL), const2),
            pl.BlockSpec((D_MODEL, d_in), const2, pipeline_mode=pl.Buffered(1)),
            pl.BlockSpec((CONV_WIDTH, LRU_WIDTH), const2),
            pl.BlockSpec((1, LRU_WIDTH), const2),
            pl.BlockSpec((LRU_BLOCKS, LRU_BLOCK, 2 * LRU_BLOCK), const3),
            pl.BlockSpec((LRU_BLOCKS, 1, 2 * LRU_BLOCK), const3),
            pl.BlockSpec((1, LRU_WIDTH), const2),
            pl.BlockSpec((1, SEC), const2),
            pl.BlockSpec((1, SEC), const2),
            pl.BlockSpec((256, 256), const2),
        ],
        out_specs=[tile, tile_t, tile, tile_t, tile],
        out_shape=[sds, sds_t, sds, sds_t, sds],
        scratch_shapes=[
            pltpu.VMEM((TM + SUBLANES, LRU_WIDTH), F32),
            pltpu.VMEM((TM, LRU_WIDTH), F32),
            pltpu.VMEM((TM, LRU_WIDTH), F32),
            pltpu.VMEM((1, LRU_WIDTH), F32),
        ],
        compiler_params=pltpu.CompilerParams(
            dimension_semantics=("arbitrary", "arbitrary"),
            vmem_limit_bytes=VMEM_LIMIT),
        name="inproj_lru_qknorm",
    )(x, ng, w_in, cw, cb, wg, bg, lam, gq, gk, gsum)


def _bucket_table():
    rel = np.arange(TA)[None, :] - np.arange(2 * TA)[:, None] + TA
    n = np.maximum(rel, 0)
    max_exact = N_BUCKETS // 2
    nf = np.maximum(n, 1).astype(np.float32)
    large = max_exact + (np.log(nf / np.float32(max_exact)) / np.float32(math.log(MAX_DISTANCE / max_exact))
                         * np.float32(N_BUCKETS - max_exact)).astype(np.int32)
    large = np.minimum(large, N_BUCKETS - 1)
    bucket = np.where(n < max_exact, n, large)
    return np.where(rel >= 0, bucket, -1).astype(np.int32)


def _attn_kernel(rb_ref, bucket_ref, lq1_ref, lk1_ref, lq2_ref, lk2_ref, sgain_ref,
                 qt_ref, k_ref, vt_ref, sg_ref, o_ref,
                 w_ref, m_ref, acc_ref):
    b = pl.program_id(0)
    i = pl.program_id(1)

    @pl.when((b == 0) & (i == 0))
    def _():
        bk = bucket_ref[...]
        for h in range(ATT_HEADS):
            far = rb_ref[N_BUCKETS - 1, h]
            w = jnp.full(bk.shape, NEG_INF, F32)
            for kk in range(N_BUCKETS):
                w = jnp.where(bk == kk, (rb_ref[kk, h] - far) * LOG2E, w)
            w_ref[h] = w

    lam = (jnp.exp(jnp.sum(lq1_ref[...] * lk1_ref[...], axis=-1, keepdims=True))
           - jnp.exp(jnp.sum(lq2_ref[...] * lk2_ref[...], axis=-1, keepdims=True))
           + LAM_INIT)

    zeros_half = jnp.zeros((HEAD_DIM, TA), BF16)
    qq = []
    for h in range(ATT_HEADS):
        q1 = qt_ref[0, h * V_DIM:h * V_DIM + HEAD_DIM, :]
        q2 = qt_ref[0, h * V_DIM + HEAD_DIM:(h + 1) * V_DIM, :]
        qq.append(jnp.concatenate(
            [jnp.concatenate([q1, zeros_half], axis=0),
             jnp.concatenate([zeros_half, q2], axis=0)], axis=1))

    ones_rows = (lax.broadcasted_iota(jnp.int32, (SUM_ROWS, TA), 0) == 0).astype(BF16)

    m_ref[...] = jnp.full(m_ref.shape, NEG_INF, F32)
    acc_ref[...] = jnp.zeros(acc_ref.shape, F32)

    def tile_update(k0, bias_rows):
        def scores(h):
            kt = k_ref[0, pl.ds(k0, TA), h * V_DIM:(h + 1) * V_DIM]
            return jnp.dot(kt, qq[h], preferred_element_type=F32)

        s_next = scores(0)
        for h in range(ATT_HEADS):
            s = s_next
            if h + 1 < ATT_HEADS:
                s_next = scores(h + 1)
            if bias_rows is not None:
                w = w_ref[h, bias_rows, :]
                s = s + jnp.concatenate([w, w], axis=1)
            m_prev = m_ref[h]
            m_new = jnp.maximum(m_prev, jnp.max(s, axis=0, keepdims=True))
            alpha = jnp.exp2(m_prev - m_new)
            p = jnp.exp2(s - m_new).astype(BF16)
            vt = vt_ref[0, h * V_DIM:(h + 1) * V_DIM, pl.ds(k0, TA)]
            vt = jnp.concatenate([vt, ones_rows], axis=0)
            acc_ref[h] = alpha * acc_ref[h] + jnp.dot(vt, p, preferred_element_type=F32)
            m_ref[h] = m_new

    def far_body(j, c):
        tile_update(pl.multiple_of(j * TA, TA), None)
        return c

    lax.fori_loop(0, i - 1, far_body, 0)

    @pl.when(i > 0)
    def _():
        tile_update(pl.multiple_of((i - 1) * TA, TA), slice(0, TA))

    tile_update(pl.multiple_of(i * TA, TA), slice(TA, 2 * TA))

    gain = sgain_ref[...] * (1.0 - LAM_INIT)
    for h in range(ATT_HEADS):
        hs = slice(h * V_DIM, (h + 1) * V_DIM)
        o = acc_ref[h, 0:V_DIM, :] * (1.0 / acc_ref[h, V_DIM:V_DIM + 1, :])
        o = o[:, 0:TA] - lam * o[:, TA:2 * TA]
        ms = jnp.mean(o * o, axis=0, keepdims=True)
        y = (o * lax.rsqrt(ms + EPS)).T * gain
        o_ref[0, :, hs] = (y * sg_ref[0, :, hs].astype(F32)).astype(BF16)


def _attention(rel_bias, lq1, lk1, lq2, lk2, sgain, qt, k, vt, sg):
    bsz, seq, width = k.shape
    bucket = jnp.asarray(_bucket_table())
    const2 = lambda b, i: (0, 0)
    return pl.pallas_call(
        _attn_kernel,
        grid=(bsz, seq // TA),
        in_specs=[
            pl.BlockSpec(memory_space=pltpu.SMEM),
            pl.BlockSpec((2 * TA, TA), const2),
            pl.BlockSpec((1, HEAD_DIM), const2),
            pl.BlockSpec((1, HEAD_DIM), const2),
            pl.BlockSpec((1, HEAD_DIM), const2),
            pl.BlockSpec((1, HEAD_DIM), const2),
            pl.BlockSpec((1, V_DIM), const2),
            pl.BlockSpec((1, width, TA), lambda b, i: (b, 0, i)),
            pl.BlockSpec((1, seq, width), lambda b, i: (b, 0, 0)),
            pl.BlockSpec((1, width, seq), lambda b, i: (b, 0, 0)),
            pl.BlockSpec((1, TA, width), lambda b, i: (b, i, 0)),
        ],
        out_specs=pl.BlockSpec((1, TA, width), lambda b, i: (b, i, 0)),
        out_shape=jax.ShapeDtypeStruct((bsz, seq, width), BF16),
        scratch_shapes=[
            pltpu.VMEM((ATT_HEADS, 2 * TA, TA), F32),
            pltpu.VMEM((ATT_HEADS, 1, 2 * TA), F32),
            pltpu.VMEM((ATT_HEADS, V_DIM + SUM_ROWS, 2 * TA), F32),
        ],
        compiler_params=pltpu.CompilerParams(
            dimension_semantics=("arbitrary", "arbitrary"),
            vmem_limit_bytes=VMEM_LIMIT),
        name="diff_attention",
    )(rel_bias, bucket, lq1, lk1, lq2, lk2, sgain, qt, k, vt, sg)


def _outproj_kernel(x_ref, ya_ref, yb_ref, w_ref, o_ref):
    acc = jnp.dot(ya_ref[...], w_ref[0:LRU_WIDTH, :], preferred_element_type=F32)
    acc = acc + jnp.dot(yb_ref[...], w_ref[LRU_WIDTH:, :], preferred_element_type=F32)
    o_ref[...] = x_ref[...] + acc


def _outproj(x2, ya, yb, w_out):
    n, d = x2.shape
    row = lambda i: (i, 0)
    return pl.pallas_call(
        _outproj_kernel,
        grid=(n // TM,),
        in_specs=[
            pl.BlockSpec((TM, d), row),
            pl.BlockSpec((TM, ya.shape[1]), row),
            pl.BlockSpec((TM, yb.shape[1]), row),
            pl.BlockSpec(w_out.shape, lambda i: (0, 0)),
        ],
        out_specs=pl.BlockSpec((TM, d), row),
        out_shape=jax.ShapeDtypeStruct((n, d), F32),
        compiler_params=pltpu.CompilerParams(
            dimension_semantics=("arbitrary",), vmem_limit_bytes=VMEM_LIMIT),
        name="outproj_residual",
    )(x2, ya, yb, w_out)


def kernel(x, norm_gain, w_in, conv_w, conv_b, w_rg, b_rg, w_ig, b_ig, lru_lambda,
           q_norm_gain, k_norm_gain, lambda_q1, lambda_k1, lambda_q2, lambda_k2,
           subln_gain, w_out, rel_bias):
    bsz, seq, d = x.shape
    assert norm_gain.shape[0] == 1, "single layer only"
    assert seq % TM == 0 and seq % TA == 0 and TM % SCAN_ROWS == 0
    n_groups = SEC // HEAD_DIM
    wg = jnp.concatenate([w_rg[0], w_ig[0]], axis=-1).astype(BF16)
    bg = jnp.concatenate([b_rg[0], b_ig[0]], axis=-1)[:, None, :]
    gq = jnp.tile(q_norm_gain[0], n_groups)[None, :]
    gk = jnp.tile(k_norm_gain[0], n_groups)[None, :]
    gsum = jnp.asarray(np.kron(np.eye(256 // HEAD_DIM), np.ones((HEAD_DIM, HEAD_DIM))), BF16)

    ylru, qt, k, vt, sg = _inproj(
        x, norm_gain, w_in[0].astype(BF16), conv_w[0], conv_b, wg, bg, lru_lambda,
        gq, gk, gsum)
    yatt = _attention(rel_bias, lambda_q1, lambda_k1, lambda_q2, lambda_k2, subln_gain,
                      qt, k, vt, sg)
    out = _outproj(x.reshape(bsz * seq, d), ylru.reshape(bsz * seq, -1),
                   yatt.reshape(bsz * seq, -1), w_out[0].astype(BF16))
    return out.reshape(bsz, seq, d)
```

```python
import math

import jax
import jax.numpy as jnp
import numpy as np
from jax import lax
from jax.experimental import pallas as pl
from jax.experimental.pallas import tpu as pltpu

F32 = jnp.float32
BF16 = jnp.bfloat16

D_MODEL = 1024
LRU_WIDTH = 1024
LRU_BLOCKS = 8
LRU_BLOCK = LRU_WIDTH // LRU_BLOCKS
CONV_WIDTH = 4
LRU_C = 8.0
ATT_HEADS = 8
HEAD_DIM = 64
V_DIM = 2 * HEAD_DIM
SEC = 1024
N_BUCKETS = 32
MAX_DISTANCE = 128
EPS = 1e-6
NEG_INF = -1e30
LAM_INIT = 0.8 - 0.6 * math.exp(-0.3 * 0)
LOG2E = math.log2(math.e)

SUBLANES = 8
TM = 512
TA = 256
SCAN_ROWS = 32
SUM_ROWS = 16
VMEM_LIMIT = 58 * 1024 * 1024


def _sigmoid(x):
    return 1.0 / (1.0 + jnp.exp(-x))


def _inproj_kernel(x_ref, ng_ref, w_ref, cw_ref, cb_ref, wg_ref, bg_ref, lam_ref,
                   gq_ref, gk_ref, gsum_ref,
                   ylru_ref, qt_ref, k_ref, vt_ref, sg_ref,
                   xbuf, a_buf, u_buf, hcar):
    t = pl.program_id(1)

    @pl.when(t == 0)
    def _():
        xbuf[0:SUBLANES, :] = jnp.zeros((SUBLANES, LRU_WIDTH), F32)
        hcar[...] = jnp.zeros_like(hcar)

    x = x_ref[0]
    ms = jnp.mean(x * x, axis=-1, keepdims=True)
    hb = (x * lax.rsqrt(ms + EPS) * ng_ref[...]).astype(BF16)

    def proj(c):
        return jnp.dot(hb, w_ref[:, c * SEC:(c + 1) * SEC], preferred_element_type=F32)

    xl = proj(0)
    xbuf[SUBLANES:SUBLANES + TM, :] = xl
    xc = cw_ref[3:4, :] * xl + cb_ref[...]
    for d in range(1, CONV_WIDTH):
        xc = xc + cw_ref[3 - d:4 - d, :] * xbuf[SUBLANES - d:SUBLANES - d + TM, :]
    xbuf[0:SUBLANES, :] = xbuf[TM:TM + SUBLANES, :]

    z = -lam_ref[...]
    neg_c_softplus = -LRU_C * (jnp.maximum(z, 0.0) + jnp.log1p(jnp.exp(-jnp.abs(z))))
    xcb = xc.astype(BF16)
    for blk in range(LRU_BLOCKS):
        cs = slice(blk * LRU_BLOCK, (blk + 1) * LRU_BLOCK)
        g = jnp.dot(xcb[:, cs], wg_ref[blk], preferred_element_type=F32) + bg_ref[blk]
        r = _sigmoid(g[:, :LRU_BLOCK])
        ig = _sigmoid(g[:, LRU_BLOCK:])
        a = jnp.exp(neg_c_softplus[:, cs] * r)
        a_buf[:, cs] = a
        u_buf[:, cs] = jnp.sqrt(1.0 - a * a) * (ig * xc[:, cs])

    row = lax.broadcasted_iota(jnp.int32, (SUBLANES, LRU_WIDTH), 0)

    def scan_body(g, carry):
        for s in range(SCAN_ROWS // SUBLANES):
            r0 = pl.multiple_of(g * SCAN_ROWS + s * SUBLANES, SUBLANES)
            a = a_buf[pl.ds(r0, SUBLANES), :]
            u = u_buf[pl.ds(r0, SUBLANES), :]
            for k in (1, 2, 4):
                keep = row >= k
                a_sh = jnp.where(keep, pltpu.roll(a, k, 0), 1.0)
                u_sh = jnp.where(keep, pltpu.roll(u, k, 0), 0.0)
                u = a * u_sh + u
                a = a * a_sh
            h = a * carry + u
            u_buf[pl.ds(r0, SUBLANES), :] = h
            carry = h[SUBLANES - 1:SUBLANES, :]
        return carry

    hcar[...] = lax.fori_loop(0, TM // SCAN_ROWS, scan_body, hcar[...])

    gl = proj(1)
    ylru_ref[0] = (u_buf[...] * (gl * _sigmoid(gl))).astype(BF16)

    def qk_norm(zc, gain):
        sq = (zc * zc).astype(BF16)
        parts = []
        for c in range(SEC // 256):
            parts.append(jnp.dot(sq[:, c * 256:(c + 1) * 256], gsum_ref[...],
                                 preferred_element_type=F32))
        ss = jnp.concatenate(parts, axis=-1)
        return zc * lax.rsqrt(ss * (1.0 / HEAD_DIM) + EPS) * gain

    qn = qk_norm(proj(2), gq_ref[...] * (LOG2E / math.sqrt(HEAD_DIM)))
    qt_ref[0] = qn.T.astype(BF16)
    k_ref[0] = qk_norm(proj(3), gk_ref[...]).astype(BF16)
    a_buf[...] = proj(4)
    vt_ref[0] = a_buf[...].T.astype(BF16)
    ga = proj(5)
    sg_ref[0] = (ga * _sigmoid(ga)).astype(BF16)


def _inproj(x, ng, w_in, cw, cb, wg, bg, lam, gq, gk, gsum):
    bsz, seq, _ = x.shape
    d_in = w_in.shape[1]
    const2 = lambda b, t: (0, 0)
    const3 = lambda b, t: (0, 0, 0)
    tile = pl.BlockSpec((1, TM, SEC), lambda b, t: (b, t, 0))
    tile_t = pl.BlockSpec((1, SEC, TM), lambda b, t: (b, 0, t))
    sds = jax.ShapeDtypeStruct((bsz, seq, SEC), BF16)
    sds_t = jax.ShapeDtypeStruct((bsz, SEC, seq), BF16)
    return pl.pallas_call(
        _inproj_kernel,
        grid=(bsz, seq // TM),
        in_specs=[
            pl.BlockSpec((1, TM, D_MODEL), lambda b, t: (b, t, 0)),
            pl.BlockSpec((1, D_MODEL), const2),
            pl.BlockSpec((D_MODEL, d_in), const2, pipeline_mode=pl.Buffered(1)),
            pl.BlockSpec((CONV_WIDTH, LRU_WIDTH), const2),
            pl.BlockSpec((1, LRU_WIDTH), const2),
            pl.BlockSpec((LRU_BLOCKS, LRU_BLOCK, 2 * LRU_BLOCK), const3),
            pl.BlockSpec((LRU_BLOCKS, 1, 2 * LRU_BLOCK), const3),
            pl.BlockSpec((1, LRU_WIDTH), const2),
            pl.BlockSpec((1, SEC), const2),
            pl.BlockSpec((1, SEC), const2),
            pl.BlockSpec((256, 256), const2),
        ],
        out_specs=[tile, tile_t, tile, tile_t, tile],
        out_shape=[sds, sds_t, sds, sds_t, sds],
        scratch_shapes=[
            pltpu.VMEM((TM + SUBLANES, LRU_WIDTH), F32),
            pltpu.VMEM((TM, LRU_WIDTH), F32),
            pltpu.VMEM((TM, LRU_WIDTH), F32),
            pltpu.VMEM((1, LRU_WIDTH), F32),
        ],
        compiler_params=pltpu.CompilerParams(
            dimension_semantics=("arbitrary", "arbitrary"),
            vmem_limit_bytes=VMEM_LIMIT),
        name="inproj_lru_qknorm",
    )(x, ng, w_in, cw, cb, wg, bg, lam, gq, gk, gsum)


def _bucket_table():
    rel = np.arange(TA)[None, :] - np.arange(2 * TA)[:, None] + TA
    n = np.maximum(rel, 0)
    max_exact = N_BUCKETS // 2
    nf = np.maximum(n, 1).astype(np.float32)
    large = max_exact + (np.log(nf / np.float32(max_exact)) / np.float32(math.log(MAX_DISTANCE / max_exact))
                         * np.float32(N_BUCKETS - max_exact)).astype(np.int32)
    large = np.minimum(large, N_BUCKETS - 1)
    bucket = np.where(n < max_exact, n, large)
    return np.where(rel >= 0, bucket, -1).astype(np.int32)


def _attn_kernel(rb_ref, bucket_ref, lq1_ref, lk1_ref, lq2_ref, lk2_ref, sgain_ref,
                 qt_ref, k_ref, vt_ref, sg_ref, o_ref,
                 w_ref, m_ref, acc_ref, sa_ref, sb_ref):
    b = pl.program_id(0)
    i = pl.program_id(1)

    @pl.when((b == 0) & (i == 0))
    def _():
        bk = bucket_ref[...]
        for h in range(ATT_HEADS):
            far = rb_ref[N_BUCKETS - 1, h]
            w = jnp.full(bk.shape, NEG_INF, F32)
            for kk in range(N_BUCKETS):
                w = jnp.where(bk == kk, (rb_ref[kk, h] - far) * LOG2E, w)
            w_ref[h] = w

    lam = (jnp.exp(jnp.sum(lq1_ref[...] * lk1_ref[...], axis=-1, keepdims=True))
           - jnp.exp(jnp.sum(lq2_ref[...] * lk2_ref[...], axis=-1, keepdims=True))
           + LAM_INIT)

    zeros_half = jnp.zeros((HEAD_DIM, TA), BF16)
    qq = []
    for h in range(ATT_HEADS):
        q1 = qt_ref[0, h * V_DIM:h * V_DIM + HEAD_DIM, :]
        q2 = qt_ref[0, h * V_DIM + HEAD_DIM:(h + 1) * V_DIM, :]
        qq.append(jnp.concatenate(
            [jnp.concatenate([q1, zeros_half], axis=0),
             jnp.concatenate([zeros_half, q2], axis=0)], axis=1))

    ones_rows = (lax.broadcasted_iota(jnp.int32, (SUM_ROWS, TA), 0) == 0).astype(BF16)

    m_ref[...] = jnp.full(m_ref.shape, NEG_INF, F32)
    acc_ref[...] = jnp.zeros(acc_ref.shape, F32)

    last_tile = k_ref.shape[1] // TA - 1

    def scores_head(h, t, buf):
        k0 = pl.multiple_of(jnp.minimum(t, last_tile) * TA, TA)
        kt = k_ref[0, pl.ds(k0, TA), h * V_DIM:(h + 1) * V_DIM]
        buf[h] = jnp.dot(kt, qq[h], preferred_element_type=F32)

    def add_bias(t, buf):
        @pl.when((t >= i - 1) & (t <= i))
        def _():
            w0 = pl.multiple_of((t - (i - 1)) * TA, TA)
            for h in range(ATT_HEADS):
                w = w_ref[h, pl.ds(w0, TA), :]
                buf[h] = buf[h] + jnp.concatenate([w, w], axis=1)

    def softmax_pv_head(h, t, buf):
        k0 = pl.multiple_of(t * TA, TA)
        s = buf[h]
        m_prev = m_ref[h]
        m_new = jnp.maximum(m_prev, jnp.max(s, axis=0, keepdims=True))
        alpha = jnp.exp2(m_prev - m_new)
        p = jnp.exp2(s - m_new).astype(BF16)
        vt = vt_ref[0, h * V_DIM:(h + 1) * V_DIM, pl.ds(k0, TA)]
        vt = jnp.concatenate([vt, ones_rows], axis=0)
        acc_ref[h] = alpha * acc_ref[h] + jnp.dot(vt, p, preferred_element_type=F32)
        m_ref[h] = m_new

    def stage(t_cur, buf_cur, t_next, buf_next):
        for h in range(ATT_HEADS):
            scores_head(h, t_next, buf_next)
            softmax_pv_head(h, t_cur, buf_cur)
        add_bias(t_next, buf_next)

    n_tiles = i + 1
    for h in range(ATT_HEADS):
        scores_head(h, 0, sa_ref)
    add_bias(0, sa_ref)

    def pair_body(u, c):
        t0 = 2 * u
        stage(t0, sa_ref, t0 + 1, sb_ref)
        stage(t0 + 1, sb_ref, t0 + 2, sa_ref)
        return c

    lax.fori_loop(0, n_tiles // 2, pair_body, 0)

    @pl.when(n_tiles % 2 == 1)
    def _():
        for h in range(ATT_HEADS):
            softmax_pv_head(h, n_tiles - 1, sa_ref)

    gain = sgain_ref[...] * (1.0 - LAM_INIT)
    for h in range(ATT_HEADS):
        hs = slice(h * V_DIM, (h + 1) * V_DIM)
        o = acc_ref[h, 0:V_DIM, :] * (1.0 / acc_ref[h, V_DIM:V_DIM + 1, :])
        o = o[:, 0:TA] - lam * o[:, TA:2 * TA]
        ms = jnp.mean(o * o, axis=0, keepdims=True)
        y = (o * lax.rsqrt(ms + EPS)).T * gain
        o_ref[0, :, hs] = (y * sg_ref[0, :, hs].astype(F32)).astype(BF16)


def _attention(rel_bias, lq1, lk1, lq2, lk2, sgain, qt, k, vt, sg):
    bsz, seq, width = k.shape
    bucket = jnp.asarray(_bucket_table())
    const2 = lambda b, i: (0, 0)
    return pl.pallas_call(
        _attn_kernel,
        grid=(bsz, seq // TA),
        in_specs=[
            pl.BlockSpec(memory_space=pltpu.SMEM),
            pl.BlockSpec((2 * TA, TA), const2),
            pl.BlockSpec((1, HEAD_DIM), const2),
            pl.BlockSpec((1, HEAD_DIM), const2),
            pl.BlockSpec((1, HEAD_DIM), const2),
            pl.BlockSpec((1, HEAD_DIM), const2),
            pl.BlockSpec((1, V_DIM), const2),
            pl.BlockSpec((1, width, TA), lambda b, i: (b, 0, i)),
            pl.BlockSpec((1, seq, width), lambda b, i: (b, 0, 0)),
            pl.BlockSpec((1, width, seq), lambda b, i: (b, 0, 0)),
            pl.BlockSpec((1, TA, width), lambda b, i: (b, i, 0)),
        ],
        out_specs=pl.BlockSpec((1, TA, width), lambda b, i: (b, i, 0)),
        out_shape=jax.ShapeDtypeStruct((bsz, seq, width), BF16),
        scratch_shapes=[
            pltpu.VMEM((ATT_HEADS, 2 * TA, TA), F32),
            pltpu.VMEM((ATT_HEADS, 1, 2 * TA), F32),
            pltpu.VMEM((ATT_HEADS, V_DIM + SUM_ROWS, 2 * TA), F32),
            pltpu.VMEM((ATT_HEADS, TA, 2 * TA), F32),
            pltpu.VMEM((ATT_HEADS, TA, 2 * TA), F32),
        ],
        compiler_params=pltpu.CompilerParams(
            dimension_semantics=("arbitrary", "arbitrary"),
            vmem_limit_bytes=VMEM_LIMIT),
        name="diff_attention",
    )(rel_bias, bucket, lq1, lk1, lq2, lk2, sgain, qt, k, vt, sg)


def _outproj_kernel(x_ref, ya_ref, yb_ref, w_ref, o_ref):
    acc = jnp.dot(ya_ref[...], w_ref[0:LRU_WIDTH, :], preferred_element_type=F32)
    acc = acc + jnp.dot(yb_ref[...], w_ref[LRU_WIDTH:, :], preferred_element_type=F32)
    o_ref[...] = x_ref[...] + acc


def _outproj(x2, ya, yb, w_out):
    n, d = x2.shape
    row = lambda i: (i, 0)
    return pl.pallas_call(
        _outproj_kernel,
        grid=(n // TM,),
        in_specs=[
            pl.BlockSpec((TM, d), row),
            pl.BlockSpec((TM, ya.shape[1]), row),
            pl.BlockSpec((TM, yb.shape[1]), row),
            pl.BlockSpec(w_out.shape, lambda i: (0, 0)),
        ],
        out_specs=pl.BlockSpec((TM, d), row),
        out_shape=jax.ShapeDtypeStruct((n, d), F32),
        compiler_params=pltpu.CompilerParams(
            dimension_semantics=("arbitrary",), vmem_limit_bytes=VMEM_LIMIT),
        name="outproj_residual",
    )(x2, ya, yb, w_out)


def kernel(x, norm_gain, w_in, conv_w, conv_b, w_rg, b_rg, w_ig, b_ig, lru_lambda,
           q_norm_gain, k_norm_gain, lambda_q1, lambda_k1, lambda_q2, lambda_k2,
           subln_gain, w_out, rel_bias):
    bsz, seq, d = x.shape
    assert norm_gain.shape[0] == 1, "single layer only"
    assert seq % TM == 0 and seq % TA == 0 and TM % SCAN_ROWS == 0
    n_groups = SEC // HEAD_DIM
    wg = jnp.concatenate([w_rg[0], w_ig[0]], axis=-1).astype(BF16)
    bg = jnp.concatenate([b_rg[0], b_ig[0]], axis=-1)[:, None, :]
    gq = jnp.tile(q_norm_gain[0], n_groups)[None, :]
    gk = jnp.tile(k_norm_gain[0], n_groups)[None, :]
    gsum = jnp.asarray(np.kron(np.eye(256 // HEAD_DIM), np.ones((HEAD_DIM, HEAD_DIM))), BF16)

    ylru, qt, k, vt, sg = _inproj(
        x, norm_gain, w_in[0].astype(BF16), conv_w[0], conv_b, wg, bg, lru_lambda,
        gq, gk, gsum)
    yatt = _attention(rel_bias, lambda_q1, lambda_k1, lambda_q2, lambda_k2, subln_gain,
                      qt, k, vt, sg)
    out = _outproj(x.reshape(bsz * seq, d), ylru.reshape(bsz * seq, -1),
                   yatt.reshape(bsz * seq, -1), w_out[0].astype(BF16))
    return out.reshape(bsz, seq, d)
```

```python
import math

import jax
import jax.numpy as jnp
import numpy as np
from jax import lax
from jax.experimental import pallas as pl
from jax.experimental.pallas import tpu as pltpu

F32 = jnp.float32
BF16 = jnp.bfloat16

D_MODEL = 1024
LRU_WIDTH = 1024
LRU_BLOCKS = 8
LRU_BLOCK = LRU_WIDTH // LRU_BLOCKS
CONV_WIDTH = 4
LRU_C = 8.0
ATT_HEADS = 8
HEAD_DIM = 64
V_DIM = 2 * HEAD_DIM
SEC = 1024
N_BUCKETS = 32
MAX_DISTANCE = 128
EPS = 1e-6
NEG_INF = -1e30
TINY = 1e-30
LAM_INIT = 0.8 - 0.6 * math.exp(-0.3 * 0)
LOG2E = math.log2(math.e)

SUBLANES = 8
TM = 512
TA = 256
SUM_ROWS = 16
VMEM_LIMIT = 58 * 1024 * 1024


def _silu(x):
    hx = 0.5 * x
    return hx + hx * jnp.tanh(hx)


def _inproj_kernel(x_ref, ng_ref, w_ref, cw_ref, cb_ref, wg_ref, bg_ref, lam_ref,
                   gq_ref, gk_ref, gsum_ref,
                   ylru_ref, qt_ref, k_ref, vt_ref, sg_ref,
                   xbuf, a_buf, u_buf, hcar):
    t = pl.program_id(1)

    @pl.when(t == 0)
    def _():
        xbuf[0:SUBLANES, :] = jnp.zeros((SUBLANES, LRU_WIDTH), F32)
        hcar[...] = jnp.zeros_like(hcar)

    x = x_ref[0]
    ms = jnp.mean(x * x, axis=-1, keepdims=True)
    hb = (x * lax.rsqrt(ms + EPS) * ng_ref[...]).astype(BF16)

    def proj(c):
        return jnp.dot(hb, w_ref[:, c * SEC:(c + 1) * SEC], preferred_element_type=F32)

    def qk_norm(zc, gain):
        sq = (zc * zc).astype(BF16)
        parts = []
        for c in range(SEC // 256):
            parts.append(jnp.dot(sq[:, c * 256:(c + 1) * 256], gsum_ref[...],
                                 preferred_element_type=F32))
        mean_sq = jnp.concatenate(parts, axis=-1)
        return zc * lax.rsqrt(mean_sq + EPS) * gain

    xl = proj(0)
    zq = proj(2)
    xbuf[SUBLANES:SUBLANES + TM, :] = xl
    xc = cw_ref[3:4, :] * xl + cb_ref[...]
    for d in range(1, CONV_WIDTH):
        xc = xc + cw_ref[3 - d:4 - d, :] * xbuf[SUBLANES - d:SUBLANES - d + TM, :]
    xbuf[0:SUBLANES, :] = xbuf[TM:TM + SUBLANES, :]

    zk = proj(3)
    z = -lam_ref[...]
    softplus = jnp.maximum(z, 0.0) + jnp.log1p(jnp.exp(-jnp.abs(z)))
    half_log2_a_max = (-0.5 * LRU_C * LOG2E) * softplus
    xcb = xc.astype(BF16)
    for blk in range(LRU_BLOCKS):
        cs = slice(blk * LRU_BLOCK, (blk + 1) * LRU_BLOCK)
        th = jnp.tanh(jnp.dot(xcb[:, cs], wg_ref[blk], preferred_element_type=F32)
                      + bg_ref[blk])
        c2 = half_log2_a_max[:, cs]
        a = jnp.exp2(c2 * th[:, :LRU_BLOCK] + c2)
        hxc = 0.5 * xc[:, cs]
        gated_x = hxc * th[:, LRU_BLOCK:] + hxc
        y = 1.0 - a * a
        a_buf[:, cs] = a
        u_buf[:, cs] = (y * lax.rsqrt(jnp.maximum(y, TINY))) * gated_x

    row = lax.broadcasted_iota(jnp.int32, (SUBLANES, LRU_WIDTH), 0)

    def scan_groups(carry, g_lo, g_hi):
        for g in range(g_lo, g_hi):
            rs = slice(g * SUBLANES, (g + 1) * SUBLANES)
            a = a_buf[rs, :]
            u = u_buf[rs, :]
            for k in (1, 2, 4):
                keep = row >= k
                a_sh = jnp.where(keep, pltpu.roll(a, k, 0), 1.0)
                u_sh = jnp.where(keep, pltpu.roll(u, k, 0), 0.0)
                u = a * u_sh + u
                a = a * a_sh
            h = a * carry + u
            u_buf[rs, :] = h
            carry = h[SUBLANES - 1:SUBLANES, :]
        return carry

    n_groups = TM // SUBLANES
    zv = proj(4)
    carry = scan_groups(hcar[...], 0, n_groups // 2)
    qt_ref[0] = qk_norm(zq, gq_ref[...] * (LOG2E / math.sqrt(HEAD_DIM))).T.astype(BF16)
    zg = proj(5)
    hcar[...] = scan_groups(carry, n_groups // 2, n_groups)
    k_ref[0] = qk_norm(zk, gk_ref[...]).astype(BF16)
    gl = proj(1)
    a_buf[...] = zv
    vt_ref[0] = a_buf[...].T.astype(BF16)
    sg_ref[0] = _silu(zg).astype(BF16)
    ylru_ref[0] = (u_buf[...] * _silu(gl)).astype(BF16)


def _inproj(x, ng, w_in, cw, cb, wg, bg, lam, gq, gk, gsum):
    bsz, seq, _ = x.shape
    d_in = w_in.shape[1]
    const2 = lambda b, t: (0, 0)
    const3 = lambda b, t: (0, 0, 0)
    tile = pl.BlockSpec((1, TM, SEC), lambda b, t: (b, t, 0))
    tile_t = pl.BlockSpec((1, SEC, TM), lambda b, t: (b, 0, t))
    sds = jax.ShapeDtypeStruct((bsz, seq, SEC), BF16)
    sds_t = jax.ShapeDtypeStruct((bsz, SEC, seq), BF16)
    return pl.pallas_call(
        _inproj_kernel,
        grid=(bsz, seq // TM),
        in_specs=[
            pl.BlockSpec((1, TM, D_MODEL), lambda b, t: (b, t, 0)),
            pl.BlockSpec((1, D_MODEL), const2),
            pl.BlockSpec((D_MODEL, d_in), const2, pipeline_mode=pl.Buffered(1)),
            pl.BlockSpec((CONV_WIDTH, LRU_WIDTH), const2),
            pl.BlockSpec((1, LRU_WIDTH), const2),
            pl.BlockSpec((LRU_BLOCKS, LRU_BLOCK, 2 * LRU_BLOCK), const3),
            pl.BlockSpec((LRU_BLOCKS, 1, 2 * LRU_BLOCK), const3),
            pl.BlockSpec((1, LRU_WIDTH), const2),
            pl.BlockSpec((1, SEC), const2),
            pl.BlockSpec((1, SEC), const2),
            pl.BlockSpec((256, 256), const2),
        ],
        out_specs=[tile, tile_t, tile, tile_t, tile],
        out_shape=[sds, sds_t, sds, sds_t, sds],
        scratch_shapes=[
            pltpu.VMEM((TM + SUBLANES, LRU_WIDTH), F32),
            pltpu.VMEM((TM, LRU_WIDTH), F32),
            pltpu.VMEM((TM, LRU_WIDTH), F32),
            pltpu.VMEM((1, LRU_WIDTH), F32),
        ],
        compiler_params=pltpu.CompilerParams(
            dimension_semantics=("arbitrary", "arbitrary"),
            vmem_limit_bytes=VMEM_LIMIT),
        name="inproj_lru_qknorm",
    )(x, ng, w_in, cw, cb, wg, bg, lam, gq, gk, gsum)


def _bucket_table():
    rel = np.arange(TA)[None, :] - np.arange(2 * TA)[:, None] + TA
    n = np.maximum(rel, 0)
    max_exact = N_BUCKETS // 2
    nf = np.maximum(n, 1).astype(np.float32)
    large = max_exact + (np.log(nf / np.float32(max_exact)) / np.float32(math.log(MAX_DISTANCE / max_exact))
                         * np.float32(N_BUCKETS - max_exact)).astype(np.int32)
    large = np.minimum(large, N_BUCKETS - 1)
    bucket = np.where(n < max_exact, n, large)
    return np.where(rel >= 0, bucket, -1).astype(np.int32)


def _attn_kernel(rb_ref, bucket_ref, lq1_ref, lk1_ref, lq2_ref, lk2_ref, sgain_ref,
                 qt_ref, k_ref, vt_ref, sg_ref, o_ref,
                 w_ref, m_ref, acc_ref, sa_ref, sb_ref):
    b = pl.program_id(0)
    i = pl.program_id(1)

    @pl.when((b == 0) & (i == 0))
    def _():
        bk = bucket_ref[...]
        for h in range(ATT_HEADS):
            far = rb_ref[N_BUCKETS - 1, h]
            w = jnp.full(bk.shape, NEG_INF, F32)
            for kk in range(N_BUCKETS):
                w = jnp.where(bk == kk, (rb_ref[kk, h] - far) * LOG2E, w)
            w_ref[h] = w

    lam = (jnp.exp(jnp.sum(lq1_ref[...] * lk1_ref[...], axis=-1, keepdims=True))
           - jnp.exp(jnp.sum(lq2_ref[...] * lk2_ref[...], axis=-1, keepdims=True))
           + LAM_INIT)

    zeros_half = jnp.zeros((HEAD_DIM, TA), BF16)
    qq = []
    for h in range(ATT_HEADS):
        q1 = qt_ref[0, h * V_DIM:h * V_DIM + HEAD_DIM, :]
        q2 = qt_ref[0, h * V_DIM + HEAD_DIM:(h + 1) * V_DIM, :]
        qq.append(jnp.concatenate(
            [jnp.concatenate([q1, zeros_half], axis=0),
             jnp.concatenate([zeros_half, q2], axis=0)], axis=1))

    ones_rows = (lax.broadcasted_iota(jnp.int32, (SUM_ROWS, TA), 0) == 0).astype(BF16)

    m_ref[...] = jnp.full(m_ref.shape, NEG_INF, F32)
    acc_ref[...] = jnp.zeros(acc_ref.shape, F32)

    last_tile = k_ref.shape[1] // TA - 1

    def scores_head(h, t, buf):
        k0 = pl.multiple_of(jnp.minimum(t, last_tile) * TA, TA)
        kt = k_ref[0, pl.ds(k0, TA), h * V_DIM:(h + 1) * V_DIM]
        buf[h] = jnp.dot(kt, qq[h], preferred_element_type=F32)

    def add_bias(t, buf):
        @pl.when((t >= i - 1) & (t <= i))
        def _():
            w0 = pl.multiple_of((t - (i - 1)) * TA, TA)
            for h in range(ATT_HEADS):
                w = w_ref[h, pl.ds(w0, TA), :]
                buf[h] = buf[h] + jnp.concatenate([w, w], axis=1)

    def softmax_pv_head(h, t, buf):
        k0 = pl.multiple_of(t * TA, TA)
        s = buf[h]
        m_prev = m_ref[h]
        m_new = jnp.maximum(m_prev, jnp.max(s, axis=0, keepdims=True))
        alpha = jnp.exp2(m_prev - m_new)
        p = jnp.exp2(s - m_new).astype(BF16)
        vt = vt_ref[0, h * V_DIM:(h + 1) * V_DIM, pl.ds(k0, TA)]
        vt = jnp.concatenate([vt, ones_rows], axis=0)
        acc_ref[h] = alpha * acc_ref[h] + jnp.dot(vt, p, preferred_element_type=F32)
        m_ref[h] = m_new

    def stage(t_cur, buf_cur, t_next, buf_next):
        for h in range(ATT_HEADS):
            scores_head(h, t_next, buf_next)
            softmax_pv_head(h, t_cur, buf_cur)
        add_bias(t_next, buf_next)

    n_tiles = i + 1
    for h in range(ATT_HEADS):
        scores_head(h, 0, sa_ref)
    add_bias(0, sa_ref)

    def pair_body(u, c):
        t0 = 2 * u
        stage(t0, sa_ref, t0 + 1, sb_ref)
        stage(t0 + 1, sb_ref, t0 + 2, sa_ref)
        return c

    lax.fori_loop(0, n_tiles // 2, pair_body, 0)

    @pl.when(n_tiles % 2 == 1)
    def _():
        for h in range(ATT_HEADS):
            softmax_pv_head(h, n_tiles - 1, sa_ref)

    gain = sgain_ref[...] * (1.0 - LAM_INIT)
    for h in range(ATT_HEADS):
        hs = slice(h * V_DIM, (h + 1) * V_DIM)
        o = acc_ref[h, 0:V_DIM, :] * (1.0 / acc_ref[h, V_DIM:V_DIM + 1, :])
        o = o[:, 0:TA] - lam * o[:, TA:2 * TA]
        ms = jnp.mean(o * o, axis=0, keepdims=True)
        y = (o * lax.rsqrt(ms + EPS)).T * gain
        o_ref[0, :, hs] = (y * sg_ref[0, :, hs].astype(F32)).astype(BF16)


def _attention(rel_bias, lq1, lk1, lq2, lk2, sgain, qt, k, vt, sg):
    bsz, seq, width = k.shape
    bucket = jnp.asarray(_bucket_table())
    const2 = lambda b, i: (0, 0)
    return pl.pallas_call(
        _attn_kernel,
        grid=(bsz, seq // TA),
        in_specs=[
            pl.BlockSpec(memory_space=pltpu.SMEM),
            pl.BlockSpec((2 * TA, TA), const2),
            pl.BlockSpec((1, HEAD_DIM), const2),
            pl.BlockSpec((1, HEAD_DIM), const2),
            pl.BlockSpec((1, HEAD_DIM), const2),
            pl.BlockSpec((1, HEAD_DIM), const2),
            pl.BlockSpec((1, V_DIM), const2),
            pl.BlockSpec((1, width, TA), lambda b, i: (b, 0, i)),
            pl.BlockSpec((1, seq, width), lambda b, i: (b, 0, 0)),
            pl.BlockSpec((1, width, seq), lambda b, i: (b, 0, 0)),
            pl.BlockSpec((1, TA, width), lambda b, i: (b, i, 0)),
        ],
        out_specs=pl.BlockSpec((1, TA, width), lambda b, i: (b, i, 0)),
        out_shape=jax.ShapeDtypeStruct((bsz, seq, width), BF16),
        scratch_shapes=[
            pltpu.VMEM((ATT_HEADS, 2 * TA, TA), F32),
            pltpu.VMEM((ATT_HEADS, 1, 2 * TA), F32),
            pltpu.VMEM((ATT_HEADS, V_DIM + SUM_ROWS, 2 * TA), F32),
            pltpu.VMEM((ATT_HEADS, TA, 2 * TA), F32),
            pltpu.VMEM((ATT_HEADS, TA, 2 * TA), F32),
        ],
        compiler_params=pltpu.CompilerParams(
            dimension_semantics=("arbitrary", "arbitrary"),
            vmem_limit_bytes=VMEM_LIMIT),
        name="diff_attention",
    )(rel_bias, bucket, lq1, lk1, lq2, lk2, sgain, qt, k, vt, sg)


def _outproj_kernel(x_ref, ya_ref, yb_ref, w_ref, o_ref):
    acc = jnp.dot(ya_ref[...], w_ref[0:LRU_WIDTH, :], preferred_element_type=F32)
    acc = acc + jnp.dot(yb_ref[...], w_ref[LRU_WIDTH:, :], preferred_element_type=F32)
    o_ref[...] = x_ref[...] + acc


def _outproj(x2, ya, yb, w_out):
    n, d = x2.shape
    row = lambda i: (i, 0)
    return pl.pallas_call(
        _outproj_kernel,
        grid=(n // TM,),
        in_specs=[
            pl.BlockSpec((TM, d), row),
            pl.BlockSpec((TM, ya.shape[1]), row),
            pl.BlockSpec((TM, yb.shape[1]), row),
            pl.BlockSpec(w_out.shape, lambda i: (0, 0)),
        ],
        out_specs=pl.BlockSpec((TM, d), row),
        out_shape=jax.ShapeDtypeStruct((n, d), F32),
        compiler_params=pltpu.CompilerParams(
            dimension_semantics=("arbitrary",), vmem_limit_bytes=VMEM_LIMIT),
        name="outproj_residual",
    )(x2, ya, yb, w_out)


def kernel(x, norm_gain, w_in, conv_w, conv_b, w_rg, b_rg, w_ig, b_ig, lru_lambda,
           q_norm_gain, k_norm_gain, lambda_q1, lambda_k1, lambda_q2, lambda_k2,
           subln_gain, w_out, rel_bias):
    bsz, seq, d = x.shape
    assert norm_gain.shape[0] == 1, "single layer only"
    assert seq % TM == 0 and seq % TA == 0
    n_groups = SEC // HEAD_DIM
    wg = (0.5 * jnp.concatenate([w_rg[0], w_ig[0]], axis=-1)).astype(BF16)
    bg = 0.5 * jnp.concatenate([b_rg[0], b_ig[0]], axis=-1)[:, None, :]
    gq = jnp.tile(q_norm_gain[0], n_groups)[None, :]
    gk = jnp.tile(k_norm_gain[0], n_groups)[None, :]
    gsum = jnp.asarray(np.kron(np.eye(256 // HEAD_DIM), np.ones((HEAD_DIM, HEAD_DIM))) / HEAD_DIM,
                       BF16)

    ylru, qt, k, vt, sg = _inproj(
        x, norm_gain, w_in[0].astype(BF16), conv_w[0], conv_b, wg, bg, lru_lambda,
        gq, gk, gsum)
    yatt = _attention(rel_bias, lambda_q1, lambda_k1, lambda_q2, lambda_k2, subln_gain,
                      qt, k, vt, sg)
    out = _outproj(x.reshape(bsz * seq, d), ylru.reshape(bsz * seq, -1),
                   yatt.reshape(bsz * seq, -1), w_out[0].astype(BF16))
    return out.reshape(bsz, seq, d)
```

```python
import math

import jax
import jax.numpy as jnp
import numpy as np
from jax import lax
from jax.experimental import pallas as pl
from jax.experimental.pallas import tpu as pltpu

F32 = jnp.float32
BF16 = jnp.bfloat16

D_MODEL = 1024
LRU_WIDTH = 1024
LRU_BLOCKS = 8
LRU_BLOCK = LRU_WIDTH // LRU_BLOCKS
CONV_WIDTH = 4
LRU_C = 8.0
ATT_HEADS = 8
HEAD_DIM = 64
V_DIM = 2 * HEAD_DIM
SEC = 1024
N_BUCKETS = 32
MAX_DISTANCE = 128
EPS = 1e-6
NEG_INF = -1e30
TINY = 1e-30
LAM_INIT = 0.8 - 0.6 * math.exp(-0.3 * 0)
LOG2E = math.log2(math.e)

SUBLANES = 8
TM = 512
TM_OUT = 1024
SLAB = 256
TA = 256
SUM_ROWS = 16
VMEM_LIMIT = 58 * 1024 * 1024


def _silu(x):
    hx = 0.5 * x
    return hx + hx * jnp.tanh(hx)


def _inproj_kernel(x_ref, ng_ref, w_ref, cw_ref, cb_ref, wg_ref, bg_ref, lam_ref,
                   gq_ref, gk_ref, gsum_ref,
                   ylru_ref, qt_ref, k_ref, vt_ref, sg_ref,
                   xbuf, a_buf, u_buf, hcar):
    t = pl.program_id(1)

    @pl.when(t == 0)
    def _():
        xbuf[0:SUBLANES, :] = jnp.zeros((SUBLANES, LRU_WIDTH), F32)
        hcar[...] = jnp.zeros_like(hcar)

    x = x_ref[0]
    ms = jnp.mean(x * x, axis=-1, keepdims=True)
    hb = (x * lax.rsqrt(ms + EPS) * ng_ref[...]).astype(BF16)

    def proj(c, s):
        col = c * SEC + s * SLAB
        return jnp.dot(hb, w_ref[:, col:col + SLAB], preferred_element_type=F32)

    def qk_norm(zc, gain):
        sq = (zc * zc).astype(BF16)
        mean_sq = jnp.dot(sq, gsum_ref[...], preferred_element_type=F32)
        return zc * lax.rsqrt(mean_sq + EPS) * gain

    z = -lam_ref[...]
    softplus = jnp.maximum(z, 0.0) + jnp.log1p(jnp.exp(-jnp.abs(z)))
    half_log2_a_max = (-0.5 * LRU_C * LOG2E) * softplus
    gq = gq_ref[...] * (LOG2E / math.sqrt(HEAD_DIM))
    row = lax.broadcasted_iota(jnp.int32, (SUBLANES, SLAB), 0)
    n_slabs = LRU_WIDTH // SLAB

    xl_next = proj(0, 0)
    for s in range(n_slabs):
        ss = slice(s * SLAB, (s + 1) * SLAB)
        xl = xl_next
        if s + 1 < n_slabs:
            xl_next = proj(0, s + 1)

        xbuf[SUBLANES:SUBLANES + TM, ss] = xl
        xc = cw_ref[3:4, ss] * xl + cb_ref[:, ss]
        for d in range(1, CONV_WIDTH):
            xc = xc + cw_ref[3 - d:4 - d, ss] * xbuf[SUBLANES - d:SUBLANES - d + TM, ss]
        xbuf[0:SUBLANES, ss] = xbuf[TM:TM + SUBLANES, ss]

        xcb = xc.astype(BF16)
        for j in range(SLAB // LRU_BLOCK):
            blk = s * (SLAB // LRU_BLOCK) + j
            cs = slice(blk * LRU_BLOCK, (blk + 1) * LRU_BLOCK)
            js = slice(j * LRU_BLOCK, (j + 1) * LRU_BLOCK)
            th = jnp.tanh(jnp.dot(xcb[:, js], wg_ref[blk], preferred_element_type=F32)
                          + bg_ref[blk])
            c2 = half_log2_a_max[:, cs]
            a = jnp.exp2(c2 * th[:, :LRU_BLOCK] + c2)
            hxc = 0.5 * xc[:, js]
            gated_x = hxc * th[:, LRU_BLOCK:] + hxc
            y = 1.0 - a * a
            a_buf[:, cs] = a
            u_buf[:, cs] = (y * lax.rsqrt(jnp.maximum(y, TINY))) * gated_x
            if j == 0:
                zq = proj(2, s)

        carry = hcar[:, ss]
        n_groups = TM // SUBLANES
        for g in range(n_groups):
            if g == 0:
                zk = proj(3, s)
            elif g == n_groups // 3:
                gl = proj(1, s)
            elif g == 2 * n_groups // 3:
                zv = proj(4, s)
            rs = slice(g * SUBLANES, (g + 1) * SUBLANES)
            a = a_buf[rs, ss]
            u = u_buf[rs, ss]
            for k in (1, 2, 4):
                keep = row >= k
                a_sh = jnp.where(keep, pltpu.roll(a, k, 0), 1.0)
                u_sh = jnp.where(keep, pltpu.roll(u, k, 0), 0.0)
                u = a * u_sh + u
                a = a * a_sh
            h = a * carry + u
            u_buf[rs, ss] = h
            carry = h[SUBLANES - 1:SUBLANES, :]
        hcar[:, ss] = carry

        zg = proj(5, s)
        qt_ref[0, ss, :] = qk_norm(zq, gq[:, ss]).T.astype(BF16)
        k_ref[0, :, ss] = qk_norm(zk, gk_ref[:, ss]).astype(BF16)
        ylru_ref[0, :, ss] = (u_buf[:, ss] * _silu(gl)).astype(BF16)
        a_buf[:, ss] = zv
        vt_ref[0, ss, :] = a_buf[:, ss].T.astype(BF16)
        sg_ref[0, :, ss] = _silu(zg).astype(BF16)


def _inproj(x, ng, w_in, cw, cb, wg, bg, lam, gq, gk, gsum):
    bsz, seq, _ = x.shape
    d_in = w_in.shape[1]
    const2 = lambda b, t: (0, 0)
    const3 = lambda b, t: (0, 0, 0)
    tile = pl.BlockSpec((1, TM, SEC), lambda b, t: (b, t, 0))
    tile_t = pl.BlockSpec((1, SEC, TM), lambda b, t: (b, 0, t))
    sds = jax.ShapeDtypeStruct((bsz, seq, SEC), BF16)
    sds_t = jax.ShapeDtypeStruct((bsz, SEC, seq), BF16)
    return pl.pallas_call(
        _inproj_kernel,
        grid=(bsz, seq // TM),
        in_specs=[
            pl.BlockSpec((1, TM, D_MODEL), lambda b, t: (b, t, 0)),
            pl.BlockSpec((1, D_MODEL), const2),
            pl.BlockSpec((D_MODEL, d_in), const2, pipeline_mode=pl.Buffered(1)),
            pl.BlockSpec((CONV_WIDTH, LRU_WIDTH), const2),
            pl.BlockSpec((1, LRU_WIDTH), const2),
            pl.BlockSpec((LRU_BLOCKS, LRU_BLOCK, 2 * LRU_BLOCK), const3),
            pl.BlockSpec((LRU_BLOCKS, 1, 2 * LRU_BLOCK), const3),
            pl.BlockSpec((1, LRU_WIDTH), const2),
            pl.BlockSpec((1, SEC), const2),
            pl.BlockSpec((1, SEC), const2),
            pl.BlockSpec((256, 256), const2),
        ],
        out_specs=[tile, tile_t, tile, tile_t, tile],
        out_shape=[sds, sds_t, sds, sds_t, sds],
        scratch_shapes=[
            pltpu.VMEM((TM + SUBLANES, LRU_WIDTH), F32),
            pltpu.VMEM((TM, LRU_WIDTH), F32),
            pltpu.VMEM((TM, LRU_WIDTH), F32),
            pltpu.VMEM((1, LRU_WIDTH), F32),
        ],
        compiler_params=pltpu.CompilerParams(
            dimension_semantics=("arbitrary", "arbitrary"),
            vmem_limit_bytes=VMEM_LIMIT),
        name="inproj_lru_qknorm",
    )(x, ng, w_in, cw, cb, wg, bg, lam, gq, gk, gsum)


def _bucket_table():
    rel = np.arange(TA)[None, :] - np.arange(2 * TA)[:, None] + TA
    n = np.maximum(rel, 0)
    max_exact = N_BUCKETS // 2
    nf = np.maximum(n, 1).astype(np.float32)
    large = max_exact + (np.log(nf / np.float32(max_exact)) / np.float32(math.log(MAX_DISTANCE / max_exact))
                         * np.float32(N_BUCKETS - max_exact)).astype(np.int32)
    large = np.minimum(large, N_BUCKETS - 1)
    bucket = np.where(n < max_exact, n, large)
    return np.where(rel >= 0, bucket, -1).astype(np.int32)


def _attn_kernel(rb_ref, bucket_ref, lq1_ref, lk1_ref, lq2_ref, lk2_ref, sgain_ref,
                 qt_ref, k_ref, vt_ref, sg_ref, o_ref,
                 w_ref, m_ref, acc_ref, sa_ref, sb_ref):
    b = pl.program_id(0)
    i = pl.program_id(1)

    @pl.when((b == 0) & (i == 0))
    def _():
        bk = bucket_ref[...]
        for h in range(ATT_HEADS):
            far = rb_ref[N_BUCKETS - 1, h]
            w = jnp.full(bk.shape, NEG_INF, F32)
            for kk in range(N_BUCKETS):
                w = jnp.where(bk == kk, (rb_ref[kk, h] - far) * LOG2E, w)
            w_ref[h] = w

    lam = (jnp.exp(jnp.sum(lq1_ref[...] * lk1_ref[...], axis=-1, keepdims=True))
           - jnp.exp(jnp.sum(lq2_ref[...] * lk2_ref[...], axis=-1, keepdims=True))
           + LAM_INIT)

    zeros_half = jnp.zeros((HEAD_DIM, TA), BF16)
    qq = []
    for h in range(ATT_HEADS):
        q1 = qt_ref[0, h * V_DIM:h * V_DIM + HEAD_DIM, :]
        q2 = qt_ref[0, h * V_DIM + HEAD_DIM:(h + 1) * V_DIM, :]
        qq.append(jnp.concatenate(
            [jnp.concatenate([q1, zeros_half], axis=0),
             jnp.concatenate([zeros_half, q2], axis=0)], axis=1))

    ones_rows = (lax.broadcasted_iota(jnp.int32, (SUM_ROWS, TA), 0) == 0).astype(BF16)

    m_ref[...] = jnp.full(m_ref.shape, NEG_INF, F32)
    acc_ref[...] = jnp.zeros(acc_ref.shape, F32)

    last_tile = k_ref.shape[1] // TA - 1

    def scores_head(h, t, buf):
        k0 = pl.multiple_of(jnp.minimum(t, last_tile) * TA, TA)
        kt = k_ref[0, pl.ds(k0, TA), h * V_DIM:(h + 1) * V_DIM]
        buf[h] = jnp.dot(kt, qq[h], preferred_element_type=F32)

    def add_bias(t, buf):
        @pl.when((t >= i - 1) & (t <= i))
        def _():
            w0 = pl.multiple_of((t - (i - 1)) * TA, TA)
            for h in range(ATT_HEADS):
                w = w_ref[h, pl.ds(w0, TA), :]
                buf[h] = buf[h] + jnp.concatenate([w, w], axis=1)

    def softmax_pv_head(h, t, buf):
        k0 = pl.multiple_of(t * TA, TA)
        s = buf[h]
        m_prev = m_ref[h]
        m_new = jnp.maximum(m_prev, jnp.max(s, axis=0, keepdims=True))
        alpha = jnp.exp2(m_prev - m_new)
        p = jnp.exp2(s - m_new).astype(BF16)
        vt = vt_ref[0, h * V_DIM:(h + 1) * V_DIM, pl.ds(k0, TA)]
        vt = jnp.concatenate([vt, ones_rows], axis=0)
        acc_ref[h] = alpha * acc_ref[h] + jnp.dot(vt, p, preferred_element_type=F32)
        m_ref[h] = m_new

    def stage(t_cur, buf_cur, t_next, buf_next):
        for h in range(ATT_HEADS):
            scores_head(h, t_next, buf_next)
            softmax_pv_head(h, t_cur, buf_cur)
        add_bias(t_next, buf_next)

    n_tiles = i + 1
    for h in range(ATT_HEADS):
        scores_head(h, 0, sa_ref)
    add_bias(0, sa_ref)

    def pair_body(u, c):
        t0 = 2 * u
        stage(t0, sa_ref, t0 + 1, sb_ref)
        stage(t0 + 1, sb_ref, t0 + 2, sa_ref)
        return c

    lax.fori_loop(0, n_tiles // 2, pair_body, 0)

    @pl.when(n_tiles % 2 == 1)
    def _():
        for h in range(ATT_HEADS):
            softmax_pv_head(h, n_tiles - 1, sa_ref)

    gain = sgain_ref[...] * (1.0 - LAM_INIT)
    for h in range(ATT_HEADS):
        hs = slice(h * V_DIM, (h + 1) * V_DIM)
        o = acc_ref[h, 0:V_DIM, :] * (1.0 / acc_ref[h, V_DIM:V_DIM + 1, :])
        o = o[:, 0:TA] - lam * o[:, TA:2 * TA]
        ms = jnp.mean(o * o, axis=0, keepdims=True)
        y = (o * lax.rsqrt(ms + EPS)).T * gain
        o_ref[0, :, hs] = (y * sg_ref[0, :, hs].astype(F32)).astype(BF16)


def _attention(rel_bias, lq1, lk1, lq2, lk2, sgain, qt, k, vt, sg):
    bsz, seq, width = k.shape
    bucket = jnp.asarray(_bucket_table())
    const2 = lambda b, i: (0, 0)
    return pl.pallas_call(
        _attn_kernel,
        grid=(bsz, seq // TA),
        in_specs=[
            pl.BlockSpec(memory_space=pltpu.SMEM),
            pl.BlockSpec((2 * TA, TA), const2),
            pl.BlockSpec((1, HEAD_DIM), const2),
            pl.BlockSpec((1, HEAD_DIM), const2),
            pl.BlockSpec((1, HEAD_DIM), const2),
            pl.BlockSpec((1, HEAD_DIM), const2),
            pl.BlockSpec((1, V_DIM), const2),
            pl.BlockSpec((1, width, TA), lambda b, i: (b, 0, i)),
            pl.BlockSpec((1, seq, width), lambda b, i: (b, 0, 0)),
            pl.BlockSpec((1, width, seq), lambda b, i: (b, 0, 0)),
            pl.BlockSpec((1, TA, width), lambda b, i: (b, i, 0)),
        ],
        out_specs=pl.BlockSpec((1, TA, width), lambda b, i: (b, i, 0)),
        out_shape=jax.ShapeDtypeStruct((bsz, seq, width), BF16),
        scratch_shapes=[
            pltpu.VMEM((ATT_HEADS, 2 * TA, TA), F32),
            pltpu.VMEM((ATT_HEADS, 1, 2 * TA), F32),
            pltpu.VMEM((ATT_HEADS, V_DIM + SUM_ROWS, 2 * TA), F32),
            pltpu.VMEM((ATT_HEADS, TA, 2 * TA), F32),
            pltpu.VMEM((ATT_HEADS, TA, 2 * TA), F32),
        ],
        compiler_params=pltpu.CompilerParams(
            dimension_semantics=("arbitrary", "arbitrary"),
            vmem_limit_bytes=VMEM_LIMIT),
        name="diff_attention",
    )(rel_bias, bucket, lq1, lk1, lq2, lk2, sgain, qt, k, vt, sg)


def _outproj_kernel(x_ref, ya_ref, yb_ref, w_ref, o_ref):
    acc = jnp.dot(ya_ref[...], w_ref[0:LRU_WIDTH, :], preferred_element_type=F32)
    acc = acc + jnp.dot(yb_ref[...], w_ref[LRU_WIDTH:, :], preferred_element_type=F32)
    o_ref[...] = x_ref[...] + acc


def _outproj(x2, ya, yb, w_out):
    n, d = x2.shape
    row = lambda i: (i, 0)
    return pl.pallas_call(
        _outproj_kernel,
        grid=(n // TM_OUT,),
        in_specs=[
            pl.BlockSpec((TM_OUT, d), row),
            pl.BlockSpec((TM_OUT, ya.shape[1]), row),
            pl.BlockSpec((TM_OUT, yb.shape[1]), row),
            pl.BlockSpec(w_out.shape, lambda i: (0, 0)),
        ],
        out_specs=pl.BlockSpec((TM_OUT, d), row),
        out_shape=jax.ShapeDtypeStruct((n, d), F32),
        compiler_params=pltpu.CompilerParams(
            dimension_semantics=("arbitrary",), vmem_limit_bytes=VMEM_LIMIT),
        name="outproj_residual",
    )(x2, ya, yb, w_out)


def kernel(x, norm_gain, w_in, conv_w, conv_b, w_rg, b_rg, w_ig, b_ig, lru_lambda,
           q_norm_gain, k_norm_gain, lambda_q1, lambda_k1, lambda_q2, lambda_k2,
           subln_gain, w_out, rel_bias):
    bsz, seq, d = x.shape
    assert norm_gain.shape[0] == 1, "single layer only"
    assert seq % TM == 0 and seq % TA == 0 and (bsz * seq) % TM_OUT == 0
    n_groups = SEC // HEAD_DIM
    wg = (0.5 * jnp.concatenate([w_rg[0], w_ig[0]], axis=-1)).astype(BF16)
    bg = 0.5 * jnp.concatenate([b_rg[0], b_ig[0]], axis=-1)[:, None, :]
    gq = jnp.tile(q_norm_gain[0], n_groups)[None, :]
    gk = jnp.tile(k_norm_gain[0], n_groups)[None, :]
    gsum = jnp.asarray(np.kron(np.eye(256 // HEAD_DIM), np.ones((HEAD_DIM, HEAD_DIM))) / HEAD_DIM,
                       BF16)

    ylru, qt, k, vt, sg = _inproj(
        x, norm_gain, w_in[0].astype(BF16), conv_w[0], conv_b, wg, bg, lru_lambda,
        gq, gk, gsum)
    yatt = _attention(rel_bias, lambda_q1, lambda_k1, lambda_q2, lambda_k2, subln_gain,
                      qt, k, vt, sg)
    out = _outproj(x.reshape(bsz * seq, d), ylru.reshape(bsz * seq, -1),
                   yatt.reshape(bsz * seq, -1), w_out[0].astype(BF16))
    return out.reshape(bsz, seq, d)
```

```python
import math

import jax
import jax.numpy as jnp
import numpy as np
from jax import lax
from jax.experimental import pallas as pl
from jax.experimental.pallas import tpu as pltpu

F32 = jnp.float32
BF16 = jnp.bfloat16

D_MODEL = 1024
LRU_WIDTH = 1024
LRU_BLOCKS = 8
LRU_BLOCK = LRU_WIDTH // LRU_BLOCKS
CONV_WIDTH = 4
LRU_C = 8.0
ATT_HEADS = 8
HEAD_DIM = 64
V_DIM = 2 * HEAD_DIM
SEC = 1024
N_BUCKETS = 32
MAX_DISTANCE = 128
EPS = 1e-6
NEG_INF = -1e30
TINY = 1e-30
LAM_INIT = 0.8 - 0.6 * math.exp(-0.3 * 0)
LOG2E = math.log2(math.e)

SUBLANES = 8
TM = 512
TM_OUT = 1024
SLAB = 256
TA = 256
SUM_ROWS = 16
VMEM_LIMIT = 58 * 1024 * 1024


def _silu(x):
    hx = 0.5 * x
    return hx + hx * jnp.tanh(hx)


def _inproj_kernel(x_ref, ng_ref, w_ref, cw_ref, cb_ref, wg_ref, bg_ref, lam_ref,
                   gq_ref, gk_ref, gsum_ref,
                   ylru_ref, qt_ref, k_ref, vt_ref, sg_ref,
                   xbuf, a_buf, u_buf, hcar):
    t = pl.program_id(1)

    @pl.when(t == 0)
    def _():
        xbuf[0:SUBLANES, :] = jnp.zeros((SUBLANES, LRU_WIDTH), F32)
        hcar[...] = jnp.zeros_like(hcar)

    x = x_ref[0]
    ms = jnp.mean(x * x, axis=-1, keepdims=True)
    hb = (x * lax.rsqrt(ms + EPS) * ng_ref[...]).astype(BF16)

    def proj(c, s):
        col = c * SEC + s * SLAB
        return jnp.dot(hb, w_ref[:, col:col + SLAB], preferred_element_type=F32)

    def qk_norm(zc, gain):
        sq = (zc * zc).astype(BF16)
        mean_sq = jnp.dot(sq, gsum_ref[...], preferred_element_type=F32)
        return zc * lax.rsqrt(mean_sq + EPS) * gain

    z = -lam_ref[...]
    softplus = jnp.maximum(z, 0.0) + jnp.log1p(jnp.exp(-jnp.abs(z)))
    half_log2_a_max = (-0.5 * LRU_C * LOG2E) * softplus
    gq = gq_ref[...] * (LOG2E / math.sqrt(HEAD_DIM))
    row = lax.broadcasted_iota(jnp.int32, (SUBLANES, SLAB), 0)
    n_slabs = LRU_WIDTH // SLAB

    xl_next = proj(0, 0)
    for s in range(n_slabs):
        ss = slice(s * SLAB, (s + 1) * SLAB)
        xl = xl_next
        if s + 1 < n_slabs:
            xl_next = proj(0, s + 1)

        xbuf[SUBLANES:SUBLANES + TM, ss] = xl
        xc = cw_ref[3:4, ss] * xl + cb_ref[:, ss]
        for d in range(1, CONV_WIDTH):
            xc = xc + cw_ref[3 - d:4 - d, ss] * xbuf[SUBLANES - d:SUBLANES - d + TM, ss]
        xbuf[0:SUBLANES, ss] = xbuf[TM:TM + SUBLANES, ss]

        xcb = xc.astype(BF16)
        for j in range(SLAB // LRU_BLOCK):
            blk = s * (SLAB // LRU_BLOCK) + j
            cs = slice(blk * LRU_BLOCK, (blk + 1) * LRU_BLOCK)
            js = slice(j * LRU_BLOCK, (j + 1) * LRU_BLOCK)
            th = jnp.tanh(jnp.dot(xcb[:, js], wg_ref[blk], preferred_element_type=F32)
                          + bg_ref[blk])
            c2 = half_log2_a_max[:, cs]
            a = jnp.exp2(c2 * th[:, :LRU_BLOCK] + c2)
            hxc = 0.5 * xc[:, js]
            gated_x = hxc * th[:, LRU_BLOCK:] + hxc
            y = 1.0 - a * a
            a_buf[:, cs] = a
            u_buf[:, cs] = (y * lax.rsqrt(jnp.maximum(y, TINY))) * gated_x
            if j == 0:
                zq = proj(2, s)

        carry = hcar[:, ss]
        n_groups = TM // SUBLANES
        for g in range(n_groups):
            if g == 0:
                zk = proj(3, s)
            elif g == n_groups // 3:
                gl = proj(1, s)
            elif g == 2 * n_groups // 3:
                zv = proj(4, s)
            rs = slice(g * SUBLANES, (g + 1) * SUBLANES)
            a = a_buf[rs, ss]
            u = u_buf[rs, ss]
            for k in (1, 2, 4):
                keep = row >= k
                a_sh = jnp.where(keep, pltpu.roll(a, k, 0), 1.0)
                u_sh = jnp.where(keep, pltpu.roll(u, k, 0), 0.0)
                u = a * u_sh + u
                a = a * a_sh
            h = a * carry + u
            u_buf[rs, ss] = h
            carry = h[SUBLANES - 1:SUBLANES, :]
        hcar[:, ss] = carry

        zg = proj(5, s)
        qt_ref[0, ss, :] = qk_norm(zq, gq[:, ss]).T.astype(BF16)
        k_ref[0, :, ss] = qk_norm(zk, gk_ref[:, ss]).astype(BF16)
        ylru_ref[0, :, ss] = (u_buf[:, ss] * _silu(gl)).astype(BF16)
        a_buf[:, ss] = zv
        vt_ref[0, ss, :] = a_buf[:, ss].T.astype(BF16)
        sg_ref[0, :, ss] = _silu(zg).astype(BF16)


def _inproj(x, ng, w_in, cw, cb, wg, bg, lam, gq, gk, gsum):
    bsz, seq, _ = x.shape
    d_in = w_in.shape[1]
    const2 = lambda b, t: (0, 0)
    const3 = lambda b, t: (0, 0, 0)
    tile = pl.BlockSpec((1, TM, SEC), lambda b, t: (b, t, 0))
    tile_t = pl.BlockSpec((1, SEC, TM), lambda b, t: (b, 0, t))
    sds = jax.ShapeDtypeStruct((bsz, seq, SEC), BF16)
    sds_t = jax.ShapeDtypeStruct((bsz, SEC, seq), BF16)
    return pl.pallas_call(
        _inproj_kernel,
        grid=(bsz, seq // TM),
        in_specs=[
            pl.BlockSpec((1, TM, D_MODEL), lambda b, t: (b, t, 0)),
            pl.BlockSpec((1, D_MODEL), const2),
            pl.BlockSpec((D_MODEL, d_in), const2, pipeline_mode=pl.Buffered(1)),
            pl.BlockSpec((CONV_WIDTH, LRU_WIDTH), const2),
            pl.BlockSpec((1, LRU_WIDTH), const2),
            pl.BlockSpec((LRU_BLOCKS, LRU_BLOCK, 2 * LRU_BLOCK), const3),
            pl.BlockSpec((LRU_BLOCKS, 1, 2 * LRU_BLOCK), const3),
            pl.BlockSpec((1, LRU_WIDTH), const2),
            pl.BlockSpec((1, SEC), const2),
            pl.BlockSpec((1, SEC), const2),
            pl.BlockSpec((256, 256), const2),
        ],
        out_specs=[tile, tile_t, tile, tile_t, tile],
        out_shape=[sds, sds_t, sds, sds_t, sds],
        scratch_shapes=[
            pltpu.VMEM((TM + SUBLANES, LRU_WIDTH), F32),
            pltpu.VMEM((TM, LRU_WIDTH), F32),
            pltpu.VMEM((TM, LRU_WIDTH), F32),
            pltpu.VMEM((1, LRU_WIDTH), F32),
        ],
        compiler_params=pltpu.CompilerParams(
            dimension_semantics=("arbitrary", "arbitrary"),
            vmem_limit_bytes=VMEM_LIMIT),
        name="inproj_lru_qknorm",
    )(x, ng, w_in, cw, cb, wg, bg, lam, gq, gk, gsum)


def _bucket_table():
    rel = np.arange(TA)[None, :] - np.arange(2 * TA)[:, None] + TA
    n = np.maximum(rel, 0)
    max_exact = N_BUCKETS // 2
    nf = np.maximum(n, 1).astype(np.float32)
    large = max_exact + (np.log(nf / np.float32(max_exact)) / np.float32(math.log(MAX_DISTANCE / max_exact))
                         * np.float32(N_BUCKETS - max_exact)).astype(np.int32)
    large = np.minimum(large, N_BUCKETS - 1)
    bucket = np.where(n < max_exact, n, large)
    return np.where(rel >= 0, bucket, -1).astype(np.int32)


def _attn_kernel(rb_ref, bucket_ref, lq1_ref, lk1_ref, lq2_ref, lk2_ref, sgain_ref,
                 qt_ref, k_ref, vt_ref, sg_ref, o_ref,
                 w_ref, m_ref, acc_ref, sa_ref, sb_ref, ma_ref, mb_ref):
    b = pl.program_id(0)
    i = pl.program_id(1)

    @pl.when((b == 0) & (i == 0))
    def _():
        bk = bucket_ref[...]
        for h in range(ATT_HEADS):
            far = rb_ref[N_BUCKETS - 1, h]
            w = jnp.full(bk.shape, NEG_INF, F32)
            for kk in range(N_BUCKETS):
                w = jnp.where(bk == kk, (rb_ref[kk, h] - far) * LOG2E, w)
            w_ref[h] = w

    lam = (jnp.exp(jnp.sum(lq1_ref[...] * lk1_ref[...], axis=-1, keepdims=True))
           - jnp.exp(jnp.sum(lq2_ref[...] * lk2_ref[...], axis=-1, keepdims=True))
           + LAM_INIT)

    zeros_half = jnp.zeros((HEAD_DIM, TA), BF16)
    qq = []
    for h in range(ATT_HEADS):
        q1 = qt_ref[0, h * V_DIM:h * V_DIM + HEAD_DIM, :]
        q2 = qt_ref[0, h * V_DIM + HEAD_DIM:(h + 1) * V_DIM, :]
        qq.append(jnp.concatenate(
            [jnp.concatenate([q1, zeros_half], axis=0),
             jnp.concatenate([zeros_half, q2], axis=0)], axis=1))

    ones_rows = (lax.broadcasted_iota(jnp.int32, (SUM_ROWS, TA), 0) == 0).astype(BF16)

    m_ref[...] = jnp.full(m_ref.shape, NEG_INF, F32)
    acc_ref[...] = jnp.zeros(acc_ref.shape, F32)

    last_tile = k_ref.shape[1] // TA - 1

    def scores_head(h, t, buf, cmax):
        k0 = pl.multiple_of(jnp.minimum(t, last_tile) * TA, TA)
        kt = k_ref[0, pl.ds(k0, TA), h * V_DIM:(h + 1) * V_DIM]
        s = jnp.dot(kt, qq[h], preferred_element_type=F32)
        buf[h] = s
        cmax[h] = jnp.max(s, axis=0, keepdims=True)

    def add_bias(t, buf, cmax):
        @pl.when((t >= i - 1) & (t <= i))
        def _():
            w0 = pl.multiple_of((t - (i - 1)) * TA, TA)
            for h in range(ATT_HEADS):
                w = w_ref[h, pl.ds(w0, TA), :]
                s = buf[h] + jnp.concatenate([w, w], axis=1)
                buf[h] = s
                cmax[h] = jnp.max(s, axis=0, keepdims=True)

    def softmax_pv_head(h, t, buf, cmax):
        k0 = pl.multiple_of(t * TA, TA)
        s = buf[h]
        m_prev = m_ref[h]
        m_new = jnp.maximum(m_prev, cmax[h])
        alpha = jnp.exp2(m_prev - m_new)
        p = jnp.exp2(s - m_new).astype(BF16)
        vt = vt_ref[0, h * V_DIM:(h + 1) * V_DIM, pl.ds(k0, TA)]
        vt = jnp.concatenate([vt, ones_rows], axis=0)
        acc_ref[h] = alpha * acc_ref[h] + jnp.dot(vt, p, preferred_element_type=F32)
        m_ref[h] = m_new

    def stage(t_cur, cur, t_next, nxt):
        for h in range(ATT_HEADS):
            scores_head(h, t_next, *nxt)
            softmax_pv_head(h, t_cur, *cur)
        add_bias(t_next, *nxt)

    buf_a = (sa_ref, ma_ref)
    buf_b = (sb_ref, mb_ref)
    n_tiles = i + 1
    for h in range(ATT_HEADS):
        scores_head(h, 0, *buf_a)
    add_bias(0, *buf_a)

    def pair_body(u, c):
        t0 = 2 * u
        stage(t0, buf_a, t0 + 1, buf_b)
        stage(t0 + 1, buf_b, t0 + 2, buf_a)
        return c

    lax.fori_loop(0, n_tiles // 2, pair_body, 0)

    @pl.when(n_tiles % 2 == 1)
    def _():
        for h in range(ATT_HEADS):
            softmax_pv_head(h, n_tiles - 1, *buf_a)

    gain = sgain_ref[...] * (1.0 - LAM_INIT)
    for h in range(ATT_HEADS):
        hs = slice(h * V_DIM, (h + 1) * V_DIM)
        o = acc_ref[h, 0:V_DIM, :] * (1.0 / acc_ref[h, V_DIM:V_DIM + 1, :])
        o = o[:, 0:TA] - lam * o[:, TA:2 * TA]
        ms = jnp.mean(o * o, axis=0, keepdims=True)
        y = (o * lax.rsqrt(ms + EPS)).T * gain
        o_ref[0, :, hs] = (y * sg_ref[0, :, hs].astype(F32)).astype(BF16)


def _attention(rel_bias, lq1, lk1, lq2, lk2, sgain, qt, k, vt, sg):
    bsz, seq, width = k.shape
    bucket = jnp.asarray(_bucket_table())
    const2 = lambda b, i: (0, 0)
    return pl.pallas_call(
        _attn_kernel,
        grid=(bsz, seq // TA),
        in_specs=[
            pl.BlockSpec(memory_space=pltpu.SMEM),
            pl.BlockSpec((2 * TA, TA), const2),
            pl.BlockSpec((1, HEAD_DIM), const2),
            pl.BlockSpec((1, HEAD_DIM), const2),
            pl.BlockSpec((1, HEAD_DIM), const2),
            pl.BlockSpec((1, HEAD_DIM), const2),
            pl.BlockSpec((1, V_DIM), const2),
            pl.BlockSpec((1, width, TA), lambda b, i: (b, 0, i)),
            pl.BlockSpec((1, seq, width), lambda b, i: (b, 0, 0)),
            pl.BlockSpec((1, width, seq), lambda b, i: (b, 0, 0)),
            pl.BlockSpec((1, TA, width), lambda b, i: (b, i, 0)),
        ],
        out_specs=pl.BlockSpec((1, TA, width), lambda b, i: (b, i, 0)),
        out_shape=jax.ShapeDtypeStruct((bsz, seq, width), BF16),
        scratch_shapes=[
            pltpu.VMEM((ATT_HEADS, 2 * TA, TA), F32),
            pltpu.VMEM((ATT_HEADS, 1, 2 * TA), F32),
            pltpu.VMEM((ATT_HEADS, V_DIM + SUM_ROWS, 2 * TA), F32),
            pltpu.VMEM((ATT_HEADS, TA, 2 * TA), F32),
            pltpu.VMEM((ATT_HEADS, TA, 2 * TA), F32),
            pltpu.VMEM((ATT_HEADS, 1, 2 * TA), F32),
            pltpu.VMEM((ATT_HEADS, 1, 2 * TA), F32),
        ],
        compiler_params=pltpu.CompilerParams(
            dimension_semantics=("arbitrary", "arbitrary"),
            vmem_limit_bytes=VMEM_LIMIT),
        name="diff_attention",
    )(rel_bias, bucket, lq1, lk1, lq2, lk2, sgain, qt, k, vt, sg)


def _outproj_kernel(x_ref, ya_ref, yb_ref, w_ref, o_ref):
    acc = jnp.dot(ya_ref[...], w_ref[0:LRU_WIDTH, :], preferred_element_type=F32)
    acc = acc + jnp.dot(yb_ref[...], w_ref[LRU_WIDTH:, :], preferred_element_type=F32)
    o_ref[...] = x_ref[...] + acc


def _outproj(x2, ya, yb, w_out):
    n, d = x2.shape
    row = lambda i: (i, 0)
    return pl.pallas_call(
        _outproj_kernel,
        grid=(n // TM_OUT,),
        in_specs=[
            pl.BlockSpec((TM_OUT, d), row),
            pl.BlockSpec((TM_OUT, ya.shape[1]), row),
            pl.BlockSpec((TM_OUT, yb.shape[1]), row),
            pl.BlockSpec(w_out.shape, lambda i: (0, 0)),
        ],
        out_specs=pl.BlockSpec((TM_OUT, d), row),
        out_shape=jax.ShapeDtypeStruct((n, d), F32),
        compiler_params=pltpu.CompilerParams(
            dimension_semantics=("arbitrary",), vmem_limit_bytes=VMEM_LIMIT),
        name="outproj_residual",
    )(x2, ya, yb, w_out)


def kernel(x, norm_gain, w_in, conv_w, conv_b, w_rg, b_rg, w_ig, b_ig, lru_lambda,
           q_norm_gain, k_norm_gain, lambda_q1, lambda_k1, lambda_q2, lambda_k2,
           subln_gain, w_out, rel_bias):
    bsz, seq, d = x.shape
    assert norm_gain.shape[0] == 1, "single layer only"
    assert seq % TM == 0 and seq % TA == 0 and (bsz * seq) % TM_OUT == 0
    n_groups = SEC // HEAD_DIM
    wg = (0.5 * jnp.concatenate([w_rg[0], w_ig[0]], axis=-1)).astype(BF16)
    bg = 0.5 * jnp.concatenate([b_rg[0], b_ig[0]], axis=-1)[:, None, :]
    gq = jnp.tile(q_norm_gain[0], n_groups)[None, :]
    gk = jnp.tile(k_norm_gain[0], n_groups)[None, :]
    gsum = jnp.asarray(np.kron(np.eye(256 // HEAD_DIM), np.ones((HEAD_DIM, HEAD_DIM))) / HEAD_DIM,
                       BF16)

    ylru, qt, k, vt, sg = _inproj(
        x, norm_gain, w_in[0].astype(BF16), conv_w[0], conv_b, wg, bg, lru_lambda,
        gq, gk, gsum)
    yatt = _attention(rel_bias, lambda_q1, lambda_k1, lambda_q2, lambda_k2, subln_gain,
                      qt, k, vt, sg)
    out = _outproj(x.reshape(bsz * seq, d), ylru.reshape(bsz * seq, -1),
                   yatt.reshape(bsz * seq, -1), w_out[0].astype(BF16))
    return out.reshape(bsz, seq, d)
```

```python
import functools
import math

import jax
import jax.numpy as jnp
import numpy as np
from jax import lax
from jax.experimental import pallas as pl
from jax.experimental.pallas import tpu as pltpu

F32 = jnp.float32
BF16 = jnp.bfloat16

D_MODEL = 1024
LRU_WIDTH = 1024
LRU_BLOCKS = 8
LRU_BLOCK = LRU_WIDTH // LRU_BLOCKS
CONV_WIDTH = 4
LRU_C = 8.0
ATT_HEADS = 8
HEAD_DIM = 64
V_DIM = 2 * HEAD_DIM
SEC = 1024
N_BUCKETS = 32
MAX_DISTANCE = 128
EPS = 1e-6
NEG_INF = -1e30
TINY = 1e-30
LAM_INIT = 0.8 - 0.6 * math.exp(-0.3 * 0)
LOG2E = math.log2(math.e)

SUBLANES = 8
TM = 512
TM_OUT = 1024
SLAB = 256
TA = 256
SUM_ROWS = 16
VMEM_LIMIT = 58 * 1024 * 1024


def _silu(x):
    hx = 0.5 * x
    return hx + hx * jnp.tanh(hx)


def _packed(x):
    return pltpu.bitcast(x.astype(BF16), jnp.uint32)


def _inproj_kernel(tiles_per_batch,
                   x_ref, ng_ref, w_ref, cw_ref, cb_ref, wg_ref, bg_ref, lam_ref,
                   gq_ref, gk_ref, gsum_ref,
                   ylru_ref, qt_ref, k_ref, vt_ref, sg_ref,
                   xbuf, a_buf, u_buf, hcar, zbuf):
    step = pl.program_id(0)

    @pl.when(step == 0)
    def _():
        zbuf[...] = jnp.zeros_like(zbuf)

    @pl.when(jnp.maximum(step - 1, 0) % tiles_per_batch == 0)
    def _():
        xbuf[0:SUBLANES, :] = jnp.zeros((SUBLANES, LRU_WIDTH), F32)
        hcar[...] = jnp.zeros_like(hcar)

    x = x_ref[0]
    ms = jnp.mean(x * x, axis=-1, keepdims=True)
    hb = (x * lax.rsqrt(ms + EPS) * ng_ref[...]).astype(BF16)

    def proj(c, s):
        col = c * SEC + s * SLAB
        prev = zbuf[:, col:col + SLAB]
        w_slab = pltpu.bitcast(w_ref[:, col:col + SLAB], BF16)
        zbuf[:, col:col + SLAB] = jnp.dot(hb, w_slab, preferred_element_type=F32)
        return prev

    def qk_norm(zc, gain):
        sq = (zc * zc).astype(BF16)
        mean_sq = jnp.dot(sq, gsum_ref[...], preferred_element_type=F32)
        return zc * lax.rsqrt(mean_sq + EPS) * gain

    z = -lam_ref[...]
    softplus = jnp.maximum(z, 0.0) + jnp.log1p(jnp.exp(-jnp.abs(z)))
    half_log2_a_max = (-0.5 * LRU_C * LOG2E) * softplus
    gq = gq_ref[...] * (LOG2E / math.sqrt(HEAD_DIM))
    row = lax.broadcasted_iota(jnp.int32, (SUBLANES, SLAB), 0)
    n_slabs = LRU_WIDTH // SLAB

    xl_next = proj(0, 0)
    for s in range(n_slabs):
        ss = slice(s * SLAB, (s + 1) * SLAB)
        xl = xl_next
        if s + 1 < n_slabs:
            xl_next = proj(0, s + 1)

        xbuf[SUBLANES:SUBLANES + TM, ss] = xl
        xc = cw_ref[3:4, ss] * xl + cb_ref[:, ss]
        for d in range(1, CONV_WIDTH):
            xc = xc + cw_ref[3 - d:4 - d, ss] * xbuf[SUBLANES - d:SUBLANES - d + TM, ss]
        xbuf[0:SUBLANES, ss] = xbuf[TM:TM + SUBLANES, ss]

        xcb = xc.astype(BF16)
        for j in range(SLAB // LRU_BLOCK):
            blk = s * (SLAB // LRU_BLOCK) + j
            cs = slice(blk * LRU_BLOCK, (blk + 1) * LRU_BLOCK)
            js = slice(j * LRU_BLOCK, (j + 1) * LRU_BLOCK)
            th = jnp.tanh(jnp.dot(xcb[:, js], wg_ref[blk], preferred_element_type=F32)
                          + bg_ref[blk])
            c2 = half_log2_a_max[:, cs]
            a = jnp.exp2(c2 * th[:, :LRU_BLOCK] + c2)
            hxc = 0.5 * xc[:, js]
            gated_x = hxc * th[:, LRU_BLOCK:] + hxc
            y = 1.0 - a * a
            a_buf[:, cs] = a
            u_buf[:, cs] = (y * lax.rsqrt(jnp.maximum(y, TINY))) * gated_x
            if j == 0:
                zq = proj(2, s)

        carry = hcar[:, ss]
        n_groups = TM // SUBLANES
        for g in range(n_groups):
            if g == 0:
                zk = proj(3, s)
            elif g == n_groups // 3:
                gl = proj(1, s)
            elif g == 2 * n_groups // 3:
                zv = proj(4, s)
            rs = slice(g * SUBLANES, (g + 1) * SUBLANES)
            a = a_buf[rs, ss]
            u = u_buf[rs, ss]
            for k in (1, 2, 4):
                keep = row >= k
                a_sh = jnp.where(keep, pltpu.roll(a, k, 0), 1.0)
                u_sh = jnp.where(keep, pltpu.roll(u, k, 0), 0.0)
                u = a * u_sh + u
                a = a * a_sh
            h = a * carry + u
            u_buf[rs, ss] = h
            carry = h[SUBLANES - 1:SUBLANES, :]
        hcar[:, ss] = carry

        zg = proj(5, s)
        qt_ref[0, s * SLAB // 2:(s + 1) * SLAB // 2, :] = _packed(qk_norm(zq, gq[:, ss]).T)
        k_ref[0, :, ss] = _packed(qk_norm(zk, gk_ref[:, ss]))
        ylru_ref[0, :, ss] = (u_buf[:, ss] * _silu(gl)).astype(BF16)
        a_buf[:, ss] = zv
        vt_ref[0, s * SLAB // 2:(s + 1) * SLAB // 2, :] = _packed(a_buf[:, ss].T)
        sg_ref[0, :, ss] = _silu(zg).astype(BF16)


def _inproj(x, ng, w_in, cw, cb, wg, bg, lam, gq, gk, gsum):
    bsz, seq, _ = x.shape
    d_in = w_in.shape[1]
    tpb = seq // TM
    n_tiles = bsz * tpb
    const2 = lambda n: (0, 0)
    const3 = lambda n: (0, 0, 0)

    def in_tile(n):
        m = jnp.minimum(n, n_tiles - 1)
        return (m // tpb, m % tpb, 0)

    def out_tile(n):
        m = jnp.maximum(n - 1, 0)
        return (m // tpb, m % tpb, 0)

    def out_tile_t(n):
        m = jnp.maximum(n - 1, 0)
        return (m // tpb, 0, m % tpb)

    tile = pl.BlockSpec((1, TM, SEC), out_tile)
    tile_p = pl.BlockSpec((1, TM // 2, SEC), out_tile)
    tile_tp = pl.BlockSpec((1, SEC // 2, TM), out_tile_t)
    sds = jax.ShapeDtypeStruct((bsz, seq, SEC), BF16)
    sds_p = jax.ShapeDtypeStruct((bsz, seq // 2, SEC), jnp.uint32)
    sds_tp = jax.ShapeDtypeStruct((bsz, SEC // 2, seq), jnp.uint32)
    return pl.pallas_call(
        functools.partial(_inproj_kernel, tpb),
        grid=(n_tiles + 1,),
        in_specs=[
            pl.BlockSpec((1, TM, D_MODEL), in_tile),
            pl.BlockSpec((1, D_MODEL), const2),
            pl.BlockSpec((D_MODEL // 2, d_in), const2, pipeline_mode=pl.Buffered(1)),
            pl.BlockSpec((CONV_WIDTH, LRU_WIDTH), const2),
            pl.BlockSpec((1, LRU_WIDTH), const2),
            pl.BlockSpec((LRU_BLOCKS, LRU_BLOCK, 2 * LRU_BLOCK), const3),
            pl.BlockSpec((LRU_BLOCKS, 1, 2 * LRU_BLOCK), const3),
            pl.BlockSpec((1, LRU_WIDTH), const2),
            pl.BlockSpec((1, SEC), const2),
            pl.BlockSpec((1, SEC), const2),
            pl.BlockSpec((256, 256), const2),
        ],
        out_specs=[tile, tile_tp, tile_p, tile_tp, tile],
        out_shape=[sds, sds_tp, sds_p, sds_tp, sds],
        scratch_shapes=[
            pltpu.VMEM((TM + SUBLANES, LRU_WIDTH), F32),
            pltpu.VMEM((TM, LRU_WIDTH), F32),
            pltpu.VMEM((TM, LRU_WIDTH), F32),
            pltpu.VMEM((1, LRU_WIDTH), F32),
            pltpu.VMEM((TM, d_in), F32),
        ],
        compiler_params=pltpu.CompilerParams(
            dimension_semantics=("arbitrary",),
            vmem_limit_bytes=VMEM_LIMIT),
        name="inproj_lru_qknorm",
    )(x, ng, w_in, cw, cb, wg, bg, lam, gq, gk, gsum)


def _bucket_table():
    rel = np.arange(TA)[None, :] - np.arange(2 * TA)[:, None] + TA
    n = np.maximum(rel, 0)
    max_exact = N_BUCKETS // 2
    nf = np.maximum(n, 1).astype(np.float32)
    large = max_exact + (np.log(nf / np.float32(max_exact)) / np.float32(math.log(MAX_DISTANCE / max_exact))
                         * np.float32(N_BUCKETS - max_exact)).astype(np.int32)
    large = np.minimum(large, N_BUCKETS - 1)
    bucket = np.where(n < max_exact, n, large)
    return np.where(rel >= 0, bucket, -1).astype(np.int32)


def _attn_kernel(rb_ref, bucket_ref, lq1_ref, lk1_ref, lq2_ref, lk2_ref, sgain_ref,
                 qt_ref, k_ref, vt_ref, sg_ref, o_ref,
                 w_ref, m_ref, acc_ref, sa_ref, sb_ref, ma_ref, mb_ref):
    b = pl.program_id(0)
    i = pl.program_id(1)

    @pl.when((b == 0) & (i == 0))
    def _():
        bk = bucket_ref[...]
        for h in range(ATT_HEADS):
            far = rb_ref[N_BUCKETS - 1, h]
            w = jnp.full(bk.shape, NEG_INF, F32)
            for kk in range(N_BUCKETS):
                w = jnp.where(bk == kk, (rb_ref[kk, h] - far) * LOG2E, w)
            w_ref[h] = w

    lam = (jnp.exp(jnp.sum(lq1_ref[...] * lk1_ref[...], axis=-1, keepdims=True))
           - jnp.exp(jnp.sum(lq2_ref[...] * lk2_ref[...], axis=-1, keepdims=True))
           + LAM_INIT)

    zeros_half = jnp.zeros((HEAD_DIM, TA), BF16)
    qq = []
    for h in range(ATT_HEADS):
        qh = pltpu.bitcast(qt_ref[0, h * HEAD_DIM:(h + 1) * HEAD_DIM, :], BF16)
        q1 = qh[0:HEAD_DIM, :]
        q2 = qh[HEAD_DIM:V_DIM, :]
        qq.append(jnp.concatenate(
            [jnp.concatenate([q1, zeros_half], axis=0),
             jnp.concatenate([zeros_half, q2], axis=0)], axis=1))

    ones_rows = (lax.broadcasted_iota(jnp.int32, (SUM_ROWS, TA), 0) == 0).astype(BF16)

    m_ref[...] = jnp.full(m_ref.shape, NEG_INF, F32)
    acc_ref[...] = jnp.zeros(acc_ref.shape, F32)

    last_tile = 2 * k_ref.shape[1] // TA - 1

    def scores_head(h, t, buf, cmax):
        k0 = pl.multiple_of(jnp.minimum(t, last_tile) * (TA // 2), TA // 2)
        kt = pltpu.bitcast(k_ref[0, pl.ds(k0, TA // 2), h * V_DIM:(h + 1) * V_DIM],
                           BF16)
        s = jnp.dot(kt, qq[h], preferred_element_type=F32)
        buf[h] = s
        cmax[h] = jnp.max(s, axis=0, keepdims=True)

    def add_bias(t, buf, cmax):
        @pl.when((t >= i - 1) & (t <= i))
        def _():
            w0 = pl.multiple_of((t - (i - 1)) * TA, TA)
            for h in range(ATT_HEADS):
                w = w_ref[h, pl.ds(w0, TA), :]
                s = buf[h] + jnp.concatenate([w, w], axis=1)
                buf[h] = s
                cmax[h] = jnp.max(s, axis=0, keepdims=True)

    def softmax_pv_head(h, t, buf, cmax):
        k0 = pl.multiple_of(t * TA, TA)
        s = buf[h]
        m_prev = m_ref[h]
        m_new = jnp.maximum(m_prev, cmax[h])
        alpha = jnp.exp2(m_prev - m_new)
        p = jnp.exp2(s - m_new).astype(BF16)
        vt = pltpu.bitcast(vt_ref[0, h * HEAD_DIM:(h + 1) * HEAD_DIM, pl.ds(k0, TA)],
                           BF16)
        vt = jnp.concatenate([vt, ones_rows], axis=0)
        acc_ref[h] = alpha * acc_ref[h] + jnp.dot(vt, p, preferred_element_type=F32)
        m_ref[h] = m_new

    def stage(t_cur, cur, t_next, nxt):
        for h in range(ATT_HEADS):
            scores_head(h, t_next, *nxt)
            softmax_pv_head(h, t_cur, *cur)
        add_bias(t_next, *nxt)

    buf_a = (sa_ref, ma_ref)
    buf_b = (sb_ref, mb_ref)
    n_tiles = i + 1
    for h in range(ATT_HEADS):
        scores_head(h, 0, *buf_a)
    add_bias(0, *buf_a)

    def pair_body(u, c):
        t0 = 2 * u
        stage(t0, buf_a, t0 + 1, buf_b)
        stage(t0 + 1, buf_b, t0 + 2, buf_a)
        return c

    lax.fori_loop(0, n_tiles // 2, pair_body, 0)

    @pl.when(n_tiles % 2 == 1)
    def _():
        for h in range(ATT_HEADS):
            softmax_pv_head(h, n_tiles - 1, *buf_a)

    gain = sgain_ref[...] * (1.0 - LAM_INIT)
    for h in range(ATT_HEADS):
        hs = slice(h * V_DIM, (h + 1) * V_DIM)
        o = acc_ref[h, 0:V_DIM, :] * (1.0 / acc_ref[h, V_DIM:V_DIM + 1, :])
        o = o[:, 0:TA] - lam * o[:, TA:2 * TA]
        ms = jnp.mean(o * o, axis=0, keepdims=True)
        y = (o * lax.rsqrt(ms + EPS)).T * gain
        o_ref[0, :, hs] = (y * sg_ref[0, :, hs].astype(F32)).astype(BF16)


def _attention(rel_bias, lq1, lk1, lq2, lk2, sgain, qt, k, vt, sg):
    bsz, seq, width = sg.shape
    bucket = jnp.asarray(_bucket_table())
    const2 = lambda b, i: (0, 0)
    return pl.pallas_call(
        _attn_kernel,
        grid=(bsz, seq // TA),
        in_specs=[
            pl.BlockSpec(memory_space=pltpu.SMEM),
            pl.BlockSpec((2 * TA, TA), const2),
            pl.BlockSpec((1, HEAD_DIM), const2),
            pl.BlockSpec((1, HEAD_DIM), const2),
            pl.BlockSpec((1, HEAD_DIM), const2),
            pl.BlockSpec((1, HEAD_DIM), const2),
            pl.BlockSpec((1, V_DIM), const2),
            pl.BlockSpec((1, width // 2, TA), lambda b, i: (b, 0, i)),
            pl.BlockSpec((1, seq // 2, width), lambda b, i: (b, 0, 0)),
            pl.BlockSpec((1, width // 2, seq), lambda b, i: (b, 0, 0)),
            pl.BlockSpec((1, TA, width), lambda b, i: (b, i, 0)),
        ],
        out_specs=pl.BlockSpec((1, TA, width), lambda b, i: (b, i, 0)),
        out_shape=jax.ShapeDtypeStruct((bsz, seq, width), BF16),
        scratch_shapes=[
            pltpu.VMEM((ATT_HEADS, 2 * TA, TA), F32),
            pltpu.VMEM((ATT_HEADS, 1, 2 * TA), F32),
            pltpu.VMEM((ATT_HEADS, V_DIM + SUM_ROWS, 2 * TA), F32),
            pltpu.VMEM((ATT_HEADS, TA, 2 * TA), F32),
            pltpu.VMEM((ATT_HEADS, TA, 2 * TA), F32),
            pltpu.VMEM((ATT_HEADS, 1, 2 * TA), F32),
            pltpu.VMEM((ATT_HEADS, 1, 2 * TA), F32),
        ],
        compiler_params=pltpu.CompilerParams(
            dimension_semantics=("arbitrary", "arbitrary"),
            vmem_limit_bytes=VMEM_LIMIT),
        name="diff_attention",
    )(rel_bias, bucket, lq1, lk1, lq2, lk2, sgain, qt, k, vt, sg)


def _outproj_kernel(x_ref, ya_ref, yb_ref, w_ref, o_ref):
    acc = jnp.dot(ya_ref[...], w_ref[0:LRU_WIDTH, :], preferred_element_type=F32)
    acc = acc + jnp.dot(yb_ref[...], w_ref[LRU_WIDTH:, :], preferred_element_type=F32)
    o_ref[...] = x_ref[...] + acc


def _outproj(x2, ya, yb, w_out):
    n, d = x2.shape
    row = lambda i: (i, 0)
    return pl.pallas_call(
        _outproj_kernel,
        grid=(n // TM_OUT,),
        in_specs=[
            pl.BlockSpec((TM_OUT, d), row),
            pl.BlockSpec((TM_OUT, ya.shape[1]), row),
            pl.BlockSpec((TM_OUT, yb.shape[1]), row),
            pl.BlockSpec(w_out.shape, lambda i: (0, 0)),
        ],
        out_specs=pl.BlockSpec((TM_OUT, d), row),
        out_shape=jax.ShapeDtypeStruct((n, d), F32),
        compiler_params=pltpu.CompilerParams(
            dimension_semantics=("arbitrary",), vmem_limit_bytes=VMEM_LIMIT),
        name="outproj_residual",
    )(x2, ya, yb, w_out)


def _pack_rows(w):
    wb = w.astype(BF16)
    pairs = jnp.stack([wb[0::2], wb[1::2]], axis=-1)
    return lax.bitcast_convert_type(pairs, jnp.uint32)


def kernel(x, norm_gain, w_in, conv_w, conv_b, w_rg, b_rg, w_ig, b_ig, lru_lambda,
           q_norm_gain, k_norm_gain, lambda_q1, lambda_k1, lambda_q2, lambda_k2,
           subln_gain, w_out, rel_bias):
    bsz, seq, d = x.shape
    assert norm_gain.shape[0] == 1, "single layer only"
    assert seq % TM == 0 and seq % TA == 0 and (bsz * seq) % TM_OUT == 0
    n_groups = SEC // HEAD_DIM
    wg = (0.5 * jnp.concatenate([w_rg[0], w_ig[0]], axis=-1)).astype(BF16)
    bg = 0.5 * jnp.concatenate([b_rg[0], b_ig[0]], axis=-1)[:, None, :]
    gq = jnp.tile(q_norm_gain[0], n_groups)[None, :]
    gk = jnp.tile(k_norm_gain[0], n_groups)[None, :]
    gsum = jnp.asarray(np.kron(np.eye(256 // HEAD_DIM), np.ones((HEAD_DIM, HEAD_DIM))) / HEAD_DIM,
                       BF16)

    ylru, qt, k, vt, sg = _inproj(
        x, norm_gain, _pack_rows(w_in[0]), conv_w[0], conv_b, wg, bg, lru_lambda,
        gq, gk, gsum)
    yatt = _attention(rel_bias, lambda_q1, lambda_k1, lambda_q2, lambda_k2, subln_gain,
                      qt, k, vt, sg)
    out = _outproj(x.reshape(bsz * seq, d), ylru.reshape(bsz * seq, -1),
                   yatt.reshape(bsz * seq, -1), w_out[0].astype(BF16))
    return out.reshape(bsz, seq, d)
```

```python
import functools
import math

import jax
import jax.numpy as jnp
import numpy as np
from jax import lax
from jax.experimental import pallas as pl
from jax.experimental.pallas import tpu as pltpu

F32 = jnp.float32
BF16 = jnp.bfloat16

D_MODEL = 1024
LRU_WIDTH = 1024
LRU_BLOCKS = 8
LRU_BLOCK = LRU_WIDTH // LRU_BLOCKS
CONV_WIDTH = 4
LRU_C = 8.0
ATT_HEADS = 8
HEAD_DIM = 64
V_DIM = 2 * HEAD_DIM
SEC = 1024
N_BUCKETS = 32
MAX_DISTANCE = 128
EPS = 1e-6
NEG_INF = -1e30
TINY = 1e-30
LAM_INIT = 0.8 - 0.6 * math.exp(-0.3 * 0)
LOG2E = math.log2(math.e)

SUBLANES = 8
TM = 512
TM_OUT = 1024
SLAB = 256
TA = 256
SUM_ROWS = 16
VMEM_LIMIT = 58 * 1024 * 1024


def _silu(x):
    hx = 0.5 * x
    return hx + hx * jnp.tanh(hx)


def _packed(x):
    return pltpu.bitcast(x.astype(BF16), jnp.uint32)


def _inproj_kernel(tiles_per_batch,
                   x_ref, ng_ref, w_ref, cw_ref, cb_ref, wg_ref, bg_ref, lam_ref,
                   gq_ref, gk_ref, gsum_ref,
                   ylru_ref, qt_ref, k_ref, vt_ref, sg_ref,
                   xbuf, a_buf, u_buf, hcar, zbuf):
    step = pl.program_id(0)

    @pl.when(step == 0)
    def _():
        zbuf[...] = jnp.zeros_like(zbuf)

    @pl.when(jnp.maximum(step - 1, 0) % tiles_per_batch == 0)
    def _():
        xbuf[0:SUBLANES, :] = jnp.zeros((SUBLANES, LRU_WIDTH), F32)
        hcar[...] = jnp.zeros_like(hcar)

    x = x_ref[0]
    ms = jnp.mean(x * x, axis=-1, keepdims=True)
    hb = (x * lax.rsqrt(ms + EPS) * ng_ref[...]).astype(BF16)

    def proj(c, s):
        col = c * SEC + s * SLAB
        prev = zbuf[:, col:col + SLAB]
        w_slab = pltpu.bitcast(w_ref[:, col:col + SLAB], BF16)
        zbuf[:, col:col + SLAB] = jnp.dot(hb, w_slab, preferred_element_type=F32)
        return prev

    def qk_norm(zc, gain):
        sq = (zc * zc).astype(BF16)
        mean_sq = jnp.dot(sq, gsum_ref[...], preferred_element_type=F32)
        return zc * lax.rsqrt(mean_sq + EPS) * gain

    z = -lam_ref[...]
    softplus = jnp.maximum(z, 0.0) + jnp.log1p(jnp.exp(-jnp.abs(z)))
    half_log2_a_max = (-0.5 * LRU_C * LOG2E) * softplus
    gq = gq_ref[...] * (LOG2E / math.sqrt(HEAD_DIM))
    row = lax.broadcasted_iota(jnp.int32, (SUBLANES, SLAB), 0)
    n_slabs = LRU_WIDTH // SLAB

    xl_next = proj(0, 0)
    for s in range(n_slabs):
        ss = slice(s * SLAB, (s + 1) * SLAB)
        xl = xl_next
        if s + 1 < n_slabs:
            xl_next = proj(0, s + 1)

        xbuf[SUBLANES:SUBLANES + TM, ss] = xl
        xc = cw_ref[3:4, ss] * xl + cb_ref[:, ss]
        for d in range(1, CONV_WIDTH):
            xc = xc + cw_ref[3 - d:4 - d, ss] * xbuf[SUBLANES - d:SUBLANES - d + TM, ss]
        xbuf[0:SUBLANES, ss] = xbuf[TM:TM + SUBLANES, ss]

        xcb = xc.astype(BF16)
        for j in range(SLAB // LRU_BLOCK):
            blk = s * (SLAB // LRU_BLOCK) + j
            cs = slice(blk * LRU_BLOCK, (blk + 1) * LRU_BLOCK)
            js = slice(j * LRU_BLOCK, (j + 1) * LRU_BLOCK)
            th = jnp.tanh(jnp.dot(xcb[:, js], wg_ref[blk], preferred_element_type=F32)
                          + bg_ref[blk])
            c2 = half_log2_a_max[:, cs]
            a = jnp.exp2(c2 * th[:, :LRU_BLOCK] + c2)
            hxc = 0.5 * xc[:, js]
            gated_x = hxc * th[:, LRU_BLOCK:] + hxc
            y = 1.0 - a * a
            a_buf[:, cs] = a
            u_buf[:, cs] = (y * lax.rsqrt(jnp.maximum(y, TINY))) * gated_x
            if j == 0:
                zq = proj(2, s)

        carry = hcar[:, ss]
        n_groups = TM // SUBLANES
        for g in range(n_groups):
            if g == 0:
                zk = proj(3, s)
            elif g == n_groups // 3:
                gl = proj(1, s)
            elif g == 2 * n_groups // 3:
                zv = proj(4, s)
            rs = slice(g * SUBLANES, (g + 1) * SUBLANES)
            a = a_buf[rs, ss]
            u = u_buf[rs, ss]
            for k in (1, 2, 4):
                keep = row >= k
                a_sh = jnp.where(keep, pltpu.roll(a, k, 0), 1.0)
                u_sh = jnp.where(keep, pltpu.roll(u, k, 0), 0.0)
                u = a * u_sh + u
                a = a * a_sh
            h = a * carry + u
            u_buf[rs, ss] = h
            carry = h[SUBLANES - 1:SUBLANES, :]
        hcar[:, ss] = carry

        zg = proj(5, s)
        qt_ref[0, s * SLAB // 2:(s + 1) * SLAB // 2, :] = _packed(qk_norm(zq, gq[:, ss]).T)
        k_ref[0, :, ss] = _packed(qk_norm(zk, gk_ref[:, ss]))
        ylru_ref[0, :, ss] = (u_buf[:, ss] * _silu(gl)).astype(BF16)
        a_buf[:, ss] = zv
        vt_ref[0, s * SLAB // 2:(s + 1) * SLAB // 2, :] = _packed(a_buf[:, ss].T)
        sg_ref[0, :, ss] = _silu(zg).astype(BF16)


def _inproj(x, ng, w_in, cw, cb, wg, bg, lam, gq, gk, gsum):
    bsz, seq, _ = x.shape
    d_in = w_in.shape[1]
    tpb = seq // TM
    n_tiles = bsz * tpb
    const2 = lambda n: (0, 0)
    const3 = lambda n: (0, 0, 0)

    def in_tile(n):
        m = jnp.minimum(n, n_tiles - 1)
        return (m // tpb, m % tpb, 0)

    def out_tile(n):
        m = jnp.maximum(n - 1, 0)
        return (m // tpb, m % tpb, 0)

    def out_tile_t(n):
        m = jnp.maximum(n - 1, 0)
        return (m // tpb, 0, m % tpb)

    tile = pl.BlockSpec((1, TM, SEC), out_tile)
    tile_p = pl.BlockSpec((1, TM // 2, SEC), out_tile)
    tile_tp = pl.BlockSpec((1, SEC // 2, TM), out_tile_t)
    sds = jax.ShapeDtypeStruct((bsz, seq, SEC), BF16)
    sds_p = jax.ShapeDtypeStruct((bsz, seq // 2, SEC), jnp.uint32)
    sds_tp = jax.ShapeDtypeStruct((bsz, SEC // 2, seq), jnp.uint32)
    return pl.pallas_call(
        functools.partial(_inproj_kernel, tpb),
        grid=(n_tiles + 1,),
        in_specs=[
            pl.BlockSpec((1, TM, D_MODEL), in_tile),
            pl.BlockSpec((1, D_MODEL), const2),
            pl.BlockSpec((D_MODEL // 2, d_in), const2, pipeline_mode=pl.Buffered(1)),
            pl.BlockSpec((CONV_WIDTH, LRU_WIDTH), const2),
            pl.BlockSpec((1, LRU_WIDTH), const2),
            pl.BlockSpec((LRU_BLOCKS, LRU_BLOCK, 2 * LRU_BLOCK), const3),
            pl.BlockSpec((LRU_BLOCKS, 1, 2 * LRU_BLOCK), const3),
            pl.BlockSpec((1, LRU_WIDTH), const2),
            pl.BlockSpec((1, SEC), const2),
            pl.BlockSpec((1, SEC), const2),
            pl.BlockSpec((256, 256), const2),
        ],
        out_specs=[tile, tile_tp, tile_p, tile_tp, tile],
        out_shape=[sds, sds_tp, sds_p, sds_tp, sds],
        scratch_shapes=[
            pltpu.VMEM((TM + SUBLANES, LRU_WIDTH), F32),
            pltpu.VMEM((TM, LRU_WIDTH), F32),
            pltpu.VMEM((TM, LRU_WIDTH), F32),
            pltpu.VMEM((1, LRU_WIDTH), F32),
            pltpu.VMEM((TM, d_in), F32),
        ],
        compiler_params=pltpu.CompilerParams(
            dimension_semantics=("arbitrary",),
            vmem_limit_bytes=VMEM_LIMIT),
        name="inproj_lru_qknorm",
    )(x, ng, w_in, cw, cb, wg, bg, lam, gq, gk, gsum)


def _bucket_table():
    rel = np.arange(TA)[None, :] - np.arange(2 * TA)[:, None] + TA
    n = np.maximum(rel, 0)
    max_exact = N_BUCKETS // 2
    nf = np.maximum(n, 1).astype(np.float32)
    large = max_exact + (np.log(nf / np.float32(max_exact)) / np.float32(math.log(MAX_DISTANCE / max_exact))
                         * np.float32(N_BUCKETS - max_exact)).astype(np.int32)
    large = np.minimum(large, N_BUCKETS - 1)
    bucket = np.where(n < max_exact, n, large)
    return np.where(rel >= 0, bucket, -1).astype(np.int32)


def _attn_kernel(rb_ref, bucket_ref, lq1_ref, lk1_ref, lq2_ref, lk2_ref, sgain_ref,
                 qt_ref, qn_ref, k_ref, vt_ref, sg_ref, o_ref,
                 w_ref, m_ref, acc_ref, sa_ref, sb_ref, ma_ref, mb_ref):
    b = pl.program_id(0)
    i = pl.program_id(1)

    @pl.when((b == 0) & (i == 0))
    def _():
        bk = bucket_ref[...]
        for h in range(ATT_HEADS):
            far = rb_ref[N_BUCKETS - 1, h]
            w = jnp.full(bk.shape, NEG_INF, F32)
            for kk in range(N_BUCKETS):
                w = jnp.where(bk == kk, (rb_ref[kk, h] - far) * LOG2E, w)
            w_ref[h] = w

    lam = (jnp.exp(jnp.sum(lq1_ref[...] * lk1_ref[...], axis=-1, keepdims=True))
           - jnp.exp(jnp.sum(lq2_ref[...] * lk2_ref[...], axis=-1, keepdims=True))
           + LAM_INIT)

    zeros_half = jnp.zeros((HEAD_DIM, TA), BF16)

    def query_operand(q_ref, h):
        qh = pltpu.bitcast(q_ref[0, h * HEAD_DIM:(h + 1) * HEAD_DIM, :], BF16)
        return jnp.concatenate(
            [jnp.concatenate([qh[0:HEAD_DIM, :], zeros_half], axis=0),
             jnp.concatenate([zeros_half, qh[HEAD_DIM:V_DIM, :]], axis=0)], axis=1)

    qq = [query_operand(qt_ref, h) for h in range(ATT_HEADS)]

    ones_rows = (lax.broadcasted_iota(jnp.int32, (SUM_ROWS, TA), 0) == 0).astype(BF16)

    m_ref[...] = jnp.full(m_ref.shape, NEG_INF, F32)
    acc_ref[...] = jnp.zeros(acc_ref.shape, F32)

    last_tile = 2 * k_ref.shape[1] // TA - 1

    def scores_head(h, t, buf, cmax, q_operand=None):
        k0 = pl.multiple_of(jnp.minimum(t, last_tile) * (TA // 2), TA // 2)
        kt = pltpu.bitcast(k_ref[0, pl.ds(k0, TA // 2), h * V_DIM:(h + 1) * V_DIM],
                           BF16)
        qop = qq[h] if q_operand is None else q_operand
        s = jnp.dot(kt, qop, preferred_element_type=F32)
        buf[h] = s
        cmax[h] = jnp.max(s, axis=0, keepdims=True)

    def add_bias(t, buf, cmax):
        @pl.when((t >= i - 1) & (t <= i))
        def _():
            w0 = pl.multiple_of((t - (i - 1)) * TA, TA)
            for h in range(ATT_HEADS):
                w = w_ref[h, pl.ds(w0, TA), :]
                s = buf[h] + jnp.concatenate([w, w], axis=1)
                buf[h] = s
                cmax[h] = jnp.max(s, axis=0, keepdims=True)

    def softmax_pv_head(h, t, buf, cmax):
        k0 = pl.multiple_of(t * TA, TA)
        s = buf[h]
        m_prev = m_ref[h]
        m_new = jnp.maximum(m_prev, cmax[h])
        alpha = jnp.exp2(m_prev - m_new)
        p = jnp.exp2(s - m_new).astype(BF16)
        vt = pltpu.bitcast(vt_ref[0, h * HEAD_DIM:(h + 1) * HEAD_DIM, pl.ds(k0, TA)],
                           BF16)
        vt = jnp.concatenate([vt, ones_rows], axis=0)
        acc_ref[h] = alpha * acc_ref[h] + jnp.dot(vt, p, preferred_element_type=F32)
        m_ref[h] = m_new

    def stage(t_cur, cur, t_next, nxt):
        for h in range(ATT_HEADS):
            scores_head(h, t_next, *nxt)
            softmax_pv_head(h, t_cur, *cur)
        add_bias(t_next, *nxt)

    buf_a = (sa_ref, ma_ref)
    buf_b = (sb_ref, mb_ref)
    n_tiles = i + 1

    @pl.when(i == 0)
    def _():
        for h in range(ATT_HEADS):
            scores_head(h, 0, *buf_a)

    add_bias(0, *buf_a)

    def pair_body(u, c):
        t0 = 2 * u
        stage(t0, buf_a, t0 + 1, buf_b)
        stage(t0 + 1, buf_b, t0 + 2, buf_a)
        return c

    lax.fori_loop(0, n_tiles // 2, pair_body, 0)

    @pl.when(n_tiles % 2 == 1)
    def _():
        for h in range(ATT_HEADS):
            softmax_pv_head(h, n_tiles - 1, *buf_a)

    gain = sgain_ref[...] * (1.0 - LAM_INIT)
    for h in range(ATT_HEADS):
        scores_head(h, 0, *buf_a, q_operand=query_operand(qn_ref, h))
        hs = slice(h * V_DIM, (h + 1) * V_DIM)
        o = acc_ref[h, 0:V_DIM, :] * (1.0 / acc_ref[h, V_DIM:V_DIM + 1, :])
        o = o[:, 0:TA] - lam * o[:, TA:2 * TA]
        ms = jnp.mean(o * o, axis=0, keepdims=True)
        y = (o * lax.rsqrt(ms + EPS)).T * gain
        o_ref[0, :, hs] = (y * sg_ref[0, :, hs].astype(F32)).astype(BF16)


def _attention(rel_bias, lq1, lk1, lq2, lk2, sgain, qt, k, vt, sg):
    bsz, seq, width = sg.shape
    bucket = jnp.asarray(_bucket_table())
    const2 = lambda b, i: (0, 0)
    return pl.pallas_call(
        _attn_kernel,
        grid=(bsz, seq // TA),
        in_specs=[
            pl.BlockSpec(memory_space=pltpu.SMEM),
            pl.BlockSpec((2 * TA, TA), const2),
            pl.BlockSpec((1, HEAD_DIM), const2),
            pl.BlockSpec((1, HEAD_DIM), const2),
            pl.BlockSpec((1, HEAD_DIM), const2),
            pl.BlockSpec((1, HEAD_DIM), const2),
            pl.BlockSpec((1, V_DIM), const2),
            pl.BlockSpec((1, width // 2, TA), lambda b, i: (b, 0, i)),
            pl.BlockSpec((1, width // 2, TA),
                         lambda b, i: (b, 0, jnp.minimum(i + 1, seq // TA - 1))),
            pl.BlockSpec((1, seq // 2, width), lambda b, i: (b, 0, 0)),
            pl.BlockSpec((1, width // 2, seq), lambda b, i: (b, 0, 0)),
            pl.BlockSpec((1, TA, width), lambda b, i: (b, i, 0)),
        ],
        out_specs=pl.BlockSpec((1, TA, width), lambda b, i: (b, i, 0)),
        out_shape=jax.ShapeDtypeStruct((bsz, seq, width), BF16),
        scratch_shapes=[
            pltpu.VMEM((ATT_HEADS, 2 * TA, TA), F32),
            pltpu.VMEM((ATT_HEADS, 1, 2 * TA), F32),
            pltpu.VMEM((ATT_HEADS, V_DIM + SUM_ROWS, 2 * TA), F32),
            pltpu.VMEM((ATT_HEADS, TA, 2 * TA), F32),
            pltpu.VMEM((ATT_HEADS, TA, 2 * TA), F32),
            pltpu.VMEM((ATT_HEADS, 1, 2 * TA), F32),
            pltpu.VMEM((ATT_HEADS, 1, 2 * TA), F32),
        ],
        compiler_params=pltpu.CompilerParams(
            dimension_semantics=("arbitrary", "arbitrary"),
            vmem_limit_bytes=VMEM_LIMIT),
        name="diff_attention",
    )(rel_bias, bucket, lq1, lk1, lq2, lk2, sgain, qt, qt, k, vt, sg)


def _outproj_kernel(x_ref, ya_ref, yb_ref, w_ref, o_ref):
    acc = jnp.dot(ya_ref[...], w_ref[0:LRU_WIDTH, :], preferred_element_type=F32)
    acc = acc + jnp.dot(yb_ref[...], w_ref[LRU_WIDTH:, :], preferred_element_type=F32)
    o_ref[...] = x_ref[...] + acc


def _outproj(x2, ya, yb, w_out):
    n, d = x2.shape
    row = lambda i: (i, 0)
    return pl.pallas_call(
        _outproj_kernel,
        grid=(n // TM_OUT,),
        in_specs=[
            pl.BlockSpec((TM_OUT, d), row),
            pl.BlockSpec((TM_OUT, ya.shape[1]), row),
            pl.BlockSpec((TM_OUT, yb.shape[1]), row),
            pl.BlockSpec(w_out.shape, lambda i: (0, 0)),
        ],
        out_specs=pl.BlockSpec((TM_OUT, d), row),
        out_shape=jax.ShapeDtypeStruct((n, d), F32),
        compiler_params=pltpu.CompilerParams(
            dimension_semantics=("arbitrary",), vmem_limit_bytes=VMEM_LIMIT),
        name="outproj_residual",
    )(x2, ya, yb, w_out)


def _pack_rows(w):
    rows, cols = w.shape
    bits = lax.bitcast_convert_type(w.astype(BF16), jnp.uint16).astype(jnp.uint32)
    bits = bits.reshape(rows // 2, 2, cols)
    return bits[:, 0, :] | (bits[:, 1, :] << 16)


def kernel(x, norm_gain, w_in, conv_w, conv_b, w_rg, b_rg, w_ig, b_ig, lru_lambda,
           q_norm_gain, k_norm_gain, lambda_q1, lambda_k1, lambda_q2, lambda_k2,
           subln_gain, w_out, rel_bias):
    bsz, seq, d = x.shape
    assert norm_gain.shape[0] == 1, "single layer only"
    assert seq % TM == 0 and seq % TA == 0 and (bsz * seq) % TM_OUT == 0
    n_groups = SEC // HEAD_DIM
    wg = (0.5 * jnp.concatenate([w_rg[0], w_ig[0]], axis=-1)).astype(BF16)
    bg = 0.5 * jnp.concatenate([b_rg[0], b_ig[0]], axis=-1)[:, None, :]
    gq = jnp.tile(q_norm_gain[0], n_groups)[None, :]
    gk = jnp.tile(k_norm_gain[0], n_groups)[None, :]
    gsum = jnp.asarray(np.kron(np.eye(256 // HEAD_DIM), np.ones((HEAD_DIM, HEAD_DIM))) / HEAD_DIM,
                       BF16)

    ylru, qt, k, vt, sg = _inproj(
        x, norm_gain, _pack_rows(w_in[0]), conv_w[0], conv_b, wg, bg, lru_lambda,
        gq, gk, gsum)
    yatt = _attention(rel_bias, lambda_q1, lambda_k1, lambda_q2, lambda_k2, subln_gain,
                      qt, k, vt, sg)
    out = _outproj(x.reshape(bsz * seq, d), ylru.reshape(bsz * seq, -1),
                   yatt.reshape(bsz * seq, -1), w_out[0].astype(BF16))
    return out.reshape(bsz, seq, d)
```

```python
import functools
import math

import jax
import jax.numpy as jnp
import numpy as np
from jax import lax
from jax.experimental import pallas as pl
from jax.experimental.pallas import tpu as pltpu

F32 = jnp.float32
BF16 = jnp.bfloat16

D_MODEL = 1024
LRU_WIDTH = 1024
LRU_BLOCKS = 8
LRU_BLOCK = LRU_WIDTH // LRU_BLOCKS
CONV_WIDTH = 4
LRU_C = 8.0
ATT_HEADS = 8
HEAD_DIM = 64
V_DIM = 2 * HEAD_DIM
SEC = 1024
N_BUCKETS = 32
MAX_DISTANCE = 128
EPS = 1e-6
NEG_INF = -1e30
TINY = 1e-30
LAM_INIT = 0.8 - 0.6 * math.exp(-0.3 * 0)
LOG2E = math.log2(math.e)

SUBLANES = 8
TM = 512
TM_OUT = 1024
SLAB = 256
PACK_COLS = 512
TA = 256
SUM_ROWS = 16
VMEM_LIMIT = 58 * 1024 * 1024


def _silu(x):
    hx = 0.5 * x
    return hx + hx * jnp.tanh(hx)


def _packed(x):
    return pltpu.bitcast(x.astype(BF16), jnp.uint32)


def _inproj_kernel(tiles_per_batch,
                   x_ref, ng_ref, w_ref, cw_ref, cb_ref, wg_ref, bg_ref, lam_ref,
                   gq_ref, gk_ref, gsum_ref,
                   ylru_ref, qt_ref, k_ref, vt_ref, sg_ref,
                   xbuf, a_buf, u_buf, hcar, zbuf):
    step = pl.program_id(0)

    @pl.when(step == 0)
    def _():
        zbuf[...] = jnp.zeros_like(zbuf)

    @pl.when(jnp.maximum(step - 1, 0) % tiles_per_batch == 0)
    def _():
        xbuf[0:SUBLANES, :] = jnp.zeros((SUBLANES, LRU_WIDTH), F32)
        hcar[...] = jnp.zeros_like(hcar)

    x = x_ref[0]
    ms = jnp.mean(x * x, axis=-1, keepdims=True)
    hb = (x * lax.rsqrt(ms + EPS) * ng_ref[...]).astype(BF16)

    def proj(c, s):
        col = c * SEC + s * SLAB
        prev = zbuf[:, col:col + SLAB]
        w_slab = pltpu.bitcast(w_ref[:, col:col + SLAB], BF16)
        zbuf[:, col:col + SLAB] = jnp.dot(hb, w_slab, preferred_element_type=F32)
        return prev

    def qk_norm(zc, gain):
        sq = (zc * zc).astype(BF16)
        mean_sq = jnp.dot(sq, gsum_ref[...], preferred_element_type=F32)
        return zc * lax.rsqrt(mean_sq + EPS) * gain

    z = -lam_ref[...]
    softplus = jnp.maximum(z, 0.0) + jnp.log1p(jnp.exp(-jnp.abs(z)))
    half_log2_a_max = (-0.5 * LRU_C * LOG2E) * softplus
    gq = gq_ref[...] * (LOG2E / math.sqrt(HEAD_DIM))
    row = lax.broadcasted_iota(jnp.int32, (SUBLANES, SLAB), 0)
    n_slabs = LRU_WIDTH // SLAB

    xl_next = proj(0, 0)
    for s in range(n_slabs):
        ss = slice(s * SLAB, (s + 1) * SLAB)
        xl = xl_next
        if s + 1 < n_slabs:
            xl_next = proj(0, s + 1)

        xbuf[SUBLANES:SUBLANES + TM, ss] = xl
        xc = cw_ref[3:4, ss] * xl + cb_ref[:, ss]
        for d in range(1, CONV_WIDTH):
            xc = xc + cw_ref[3 - d:4 - d, ss] * xbuf[SUBLANES - d:SUBLANES - d + TM, ss]
        xbuf[0:SUBLANES, ss] = xbuf[TM:TM + SUBLANES, ss]

        xcb = xc.astype(BF16)
        for j in range(SLAB // LRU_BLOCK):
            blk = s * (SLAB // LRU_BLOCK) + j
            cs = slice(blk * LRU_BLOCK, (blk + 1) * LRU_BLOCK)
            js = slice(j * LRU_BLOCK, (j + 1) * LRU_BLOCK)
            th = jnp.tanh(jnp.dot(xcb[:, js], wg_ref[blk], preferred_element_type=F32)
                          + bg_ref[blk])
            c2 = half_log2_a_max[:, cs]
            a = jnp.exp2(c2 * th[:, :LRU_BLOCK] + c2)
            hxc = 0.5 * xc[:, js]
            gated_x = hxc * th[:, LRU_BLOCK:] + hxc
            y = 1.0 - a * a
            a_buf[:, cs] = a
            u_buf[:, cs] = (y * lax.rsqrt(jnp.maximum(y, TINY))) * gated_x
            if j == 0:
                zq = proj(2, s)

        carry = hcar[:, ss]
        n_groups = TM // SUBLANES
        for g in range(n_groups):
            if g == 0:
                zk = proj(3, s)
            elif g == n_groups // 3:
                gl = proj(1, s)
            elif g == 2 * n_groups // 3:
                zv = proj(4, s)
            rs = slice(g * SUBLANES, (g + 1) * SUBLANES)
            a = a_buf[rs, ss]
            u = u_buf[rs, ss]
            for k in (1, 2, 4):
                keep = row >= k
                a_sh = jnp.where(keep, pltpu.roll(a, k, 0), 1.0)
                u_sh = jnp.where(keep, pltpu.roll(u, k, 0), 0.0)
                u = a * u_sh + u
                a = a * a_sh
            h = a * carry + u
            u_buf[rs, ss] = h
            carry = h[SUBLANES - 1:SUBLANES, :]
        hcar[:, ss] = carry

        zg = proj(5, s)
        qt_ref[0, s * SLAB // 2:(s + 1) * SLAB // 2, :] = _packed(qk_norm(zq, gq[:, ss]).T)
        k_ref[0, :, ss] = _packed(qk_norm(zk, gk_ref[:, ss]))
        ylru_ref[0, :, ss] = (u_buf[:, ss] * _silu(gl)).astype(BF16)
        a_buf[:, ss] = zv
        vt_ref[0, s * SLAB // 2:(s + 1) * SLAB // 2, :] = _packed(a_buf[:, ss].T)
        sg_ref[0, :, ss] = _silu(zg).astype(BF16)


def _inproj(x, ng, w_in, cw, cb, wg, bg, lam, gq, gk, gsum):
    bsz, seq, _ = x.shape
    d_in = w_in.shape[1]
    tpb = seq // TM
    n_tiles = bsz * tpb
    const2 = lambda n: (0, 0)
    const3 = lambda n: (0, 0, 0)

    def in_tile(n):
        m = jnp.minimum(n, n_tiles - 1)
        return (m // tpb, m % tpb, 0)

    def out_tile(n):
        m = jnp.maximum(n - 1, 0)
        return (m // tpb, m % tpb, 0)

    def out_tile_t(n):
        m = jnp.maximum(n - 1, 0)
        return (m // tpb, 0, m % tpb)

    tile = pl.BlockSpec((1, TM, SEC), out_tile)
    tile_p = pl.BlockSpec((1, TM // 2, SEC), out_tile)
    tile_tp = pl.BlockSpec((1, SEC // 2, TM), out_tile_t)
    sds = jax.ShapeDtypeStruct((bsz, seq, SEC), BF16)
    sds_p = jax.ShapeDtypeStruct((bsz, seq // 2, SEC), jnp.uint32)
    sds_tp = jax.ShapeDtypeStruct((bsz, SEC // 2, seq), jnp.uint32)
    return pl.pallas_call(
        functools.partial(_inproj_kernel, tpb),
        grid=(n_tiles + 1,),
        in_specs=[
            pl.BlockSpec((1, TM, D_MODEL), in_tile),
            pl.BlockSpec((1, D_MODEL), const2),
            pl.BlockSpec((D_MODEL // 2, d_in), const2, pipeline_mode=pl.Buffered(1)),
            pl.BlockSpec((CONV_WIDTH, LRU_WIDTH), const2),
            pl.BlockSpec((1, LRU_WIDTH), const2),
            pl.BlockSpec((LRU_BLOCKS, LRU_BLOCK, 2 * LRU_BLOCK), const3),
            pl.BlockSpec((LRU_BLOCKS, 1, 2 * LRU_BLOCK), const3),
            pl.BlockSpec((1, LRU_WIDTH), const2),
            pl.BlockSpec((1, SEC), const2),
            pl.BlockSpec((1, SEC), const2),
            pl.BlockSpec((256, 256), const2),
        ],
        out_specs=[tile, tile_tp, tile_p, tile_tp, tile],
        out_shape=[sds, sds_tp, sds_p, sds_tp, sds],
        scratch_shapes=[
            pltpu.VMEM((TM + SUBLANES, LRU_WIDTH), F32),
            pltpu.VMEM((TM, LRU_WIDTH), F32),
            pltpu.VMEM((TM, LRU_WIDTH), F32),
            pltpu.VMEM((1, LRU_WIDTH), F32),
            pltpu.VMEM((TM, d_in), F32),
        ],
        compiler_params=pltpu.CompilerParams(
            dimension_semantics=("arbitrary",),
            vmem_limit_bytes=VMEM_LIMIT),
        name="inproj_lru_qknorm",
    )(x, ng, w_in, cw, cb, wg, bg, lam, gq, gk, gsum)


def _bucket_table():
    rel = np.arange(TA)[None, :] - np.arange(2 * TA)[:, None] + TA
    n = np.maximum(rel, 0)
    max_exact = N_BUCKETS // 2
    nf = np.maximum(n, 1).astype(np.float32)
    large = max_exact + (np.log(nf / np.float32(max_exact)) / np.float32(math.log(MAX_DISTANCE / max_exact))
                         * np.float32(N_BUCKETS - max_exact)).astype(np.int32)
    large = np.minimum(large, N_BUCKETS - 1)
    bucket = np.where(n < max_exact, n, large)
    return np.where(rel >= 0, bucket, -1).astype(np.int32)


def _attn_kernel(rb_ref, bucket_ref, lq1_ref, lk1_ref, lq2_ref, lk2_ref, sgain_ref,
                 qt_ref, qn_ref, k_ref, vt_ref, sg_ref, o_ref,
                 w_ref, m_ref, acc_ref, sa_ref, sb_ref, ma_ref, mb_ref):
    b = pl.program_id(0)
    i = pl.program_id(1)

    @pl.when((b == 0) & (i == 0))
    def _():
        bk = bucket_ref[...]
        for h in range(ATT_HEADS):
            far = rb_ref[N_BUCKETS - 1, h]
            w = jnp.full(bk.shape, NEG_INF, F32)
            for kk in range(N_BUCKETS):
                w = jnp.where(bk == kk, (rb_ref[kk, h] - far) * LOG2E, w)
            w_ref[h] = w

    lam = (jnp.exp(jnp.sum(lq1_ref[...] * lk1_ref[...], axis=-1, keepdims=True))
           - jnp.exp(jnp.sum(lq2_ref[...] * lk2_ref[...], axis=-1, keepdims=True))
           + LAM_INIT)

    zeros_half = jnp.zeros((HEAD_DIM, TA), BF16)

    def query_operand(q_ref, h):
        qh = pltpu.bitcast(q_ref[0, h * HEAD_DIM:(h + 1) * HEAD_DIM, :], BF16)
        return jnp.concatenate(
            [jnp.concatenate([qh[0:HEAD_DIM, :], zeros_half], axis=0),
             jnp.concatenate([zeros_half, qh[HEAD_DIM:V_DIM, :]], axis=0)], axis=1)

    qq = [query_operand(qt_ref, h) for h in range(ATT_HEADS)]

    ones_rows = (lax.broadcasted_iota(jnp.int32, (SUM_ROWS, TA), 0) == 0).astype(BF16)

    m_ref[...] = jnp.full(m_ref.shape, NEG_INF, F32)
    acc_ref[...] = jnp.zeros(acc_ref.shape, F32)

    last_tile = 2 * k_ref.shape[1] // TA - 1

    def scores_head(h, t, buf, cmax, q_operand=None):
        k0 = pl.multiple_of(jnp.minimum(t, last_tile) * (TA // 2), TA // 2)
        kt = pltpu.bitcast(k_ref[0, pl.ds(k0, TA // 2), h * V_DIM:(h + 1) * V_DIM],
                           BF16)
        qop = qq[h] if q_operand is None else q_operand
        s = jnp.dot(kt, qop, preferred_element_type=F32)
        buf[h] = s
        cmax[h] = jnp.max(s, axis=0, keepdims=True)

    def add_bias(t, buf, cmax):
        @pl.when((t >= i - 1) & (t <= i))
        def _():
            w0 = pl.multiple_of((t - (i - 1)) * TA, TA)
            for h in range(ATT_HEADS):
                w = w_ref[h, pl.ds(w0, TA), :]
                s = buf[h] + jnp.concatenate([w, w], axis=1)
                buf[h] = s
                cmax[h] = jnp.max(s, axis=0, keepdims=True)

    def softmax_pv_head(h, t, buf, cmax):
        k0 = pl.multiple_of(t * TA, TA)
        s = buf[h]
        m_prev = m_ref[h]
        m_new = jnp.maximum(m_prev, cmax[h])
        alpha = jnp.exp2(m_prev - m_new)
        p = jnp.exp2(s - m_new).astype(BF16)
        vt = pltpu.bitcast(vt_ref[0, h * HEAD_DIM:(h + 1) * HEAD_DIM, pl.ds(k0, TA)],
                           BF16)
        vt = jnp.concatenate([vt, ones_rows], axis=0)
        acc_ref[h] = alpha * acc_ref[h] + jnp.dot(vt, p, preferred_element_type=F32)
        m_ref[h] = m_new

    def stage(t_cur, cur, t_next, nxt):
        for h in range(ATT_HEADS):
            scores_head(h, t_next, *nxt)
            softmax_pv_head(h, t_cur, *cur)
        add_bias(t_next, *nxt)

    buf_a = (sa_ref, ma_ref)
    buf_b = (sb_ref, mb_ref)
    n_tiles = i + 1

    @pl.when(i == 0)
    def _():
        for h in range(ATT_HEADS):
            scores_head(h, 0, *buf_a)

    add_bias(0, *buf_a)

    def pair_body(u, c):
        t0 = 2 * u
        stage(t0, buf_a, t0 + 1, buf_b)
        stage(t0 + 1, buf_b, t0 + 2, buf_a)
        return c

    lax.fori_loop(0, n_tiles // 2, pair_body, 0)

    @pl.when(n_tiles % 2 == 1)
    def _():
        for h in range(ATT_HEADS):
            softmax_pv_head(h, n_tiles - 1, *buf_a)

    gain = sgain_ref[...] * (1.0 - LAM_INIT)
    for h in range(ATT_HEADS):
        scores_head(h, 0, *buf_a, q_operand=query_operand(qn_ref, h))
        hs = slice(h * V_DIM, (h + 1) * V_DIM)
        o = acc_ref[h, 0:V_DIM, :] * (1.0 / acc_ref[h, V_DIM:V_DIM + 1, :])
        o = o[:, 0:TA] - lam * o[:, TA:2 * TA]
        ms = jnp.mean(o * o, axis=0, keepdims=True)
        y = (o * lax.rsqrt(ms + EPS)).T * gain
        o_ref[0, :, hs] = (y * sg_ref[0, :, hs].astype(F32)).astype(BF16)


def _attention(rel_bias, lq1, lk1, lq2, lk2, sgain, qt, k, vt, sg):
    bsz, seq, width = sg.shape
    bucket = jnp.asarray(_bucket_table())
    const2 = lambda b, i: (0, 0)
    return pl.pallas_call(
        _attn_kernel,
        grid=(bsz, seq // TA),
        in_specs=[
            pl.BlockSpec(memory_space=pltpu.SMEM),
            pl.BlockSpec((2 * TA, TA), const2),
            pl.BlockSpec((1, HEAD_DIM), const2),
            pl.BlockSpec((1, HEAD_DIM), const2),
            pl.BlockSpec((1, HEAD_DIM), const2),
            pl.BlockSpec((1, HEAD_DIM), const2),
            pl.BlockSpec((1, V_DIM), const2),
            pl.BlockSpec((1, width // 2, TA), lambda b, i: (b, 0, i)),
            pl.BlockSpec((1, width // 2, TA),
                         lambda b, i: (b, 0, jnp.minimum(i + 1, seq // TA - 1))),
            pl.BlockSpec((1, seq // 2, width), lambda b, i: (b, 0, 0)),
            pl.BlockSpec((1, width // 2, seq), lambda b, i: (b, 0, 0)),
            pl.BlockSpec((1, TA, width), lambda b, i: (b, i, 0)),
        ],
        out_specs=pl.BlockSpec((1, TA, width), lambda b, i: (b, i, 0)),
        out_shape=jax.ShapeDtypeStruct((bsz, seq, width), BF16),
        scratch_shapes=[
            pltpu.VMEM((ATT_HEADS, 2 * TA, TA), F32),
            pltpu.VMEM((ATT_HEADS, 1, 2 * TA), F32),
            pltpu.VMEM((ATT_HEADS, V_DIM + SUM_ROWS, 2 * TA), F32),
            pltpu.VMEM((ATT_HEADS, TA, 2 * TA), F32),
            pltpu.VMEM((ATT_HEADS, TA, 2 * TA), F32),
            pltpu.VMEM((ATT_HEADS, 1, 2 * TA), F32),
            pltpu.VMEM((ATT_HEADS, 1, 2 * TA), F32),
        ],
        compiler_params=pltpu.CompilerParams(
            dimension_semantics=("arbitrary", "arbitrary"),
            vmem_limit_bytes=VMEM_LIMIT),
        name="diff_attention",
    )(rel_bias, bucket, lq1, lk1, lq2, lk2, sgain, qt, qt, k, vt, sg)


def _outproj_kernel(x_ref, ya_ref, yb_ref, w_ref, o_ref):
    acc = jnp.dot(ya_ref[...], w_ref[0:LRU_WIDTH, :], preferred_element_type=F32)
    acc = acc + jnp.dot(yb_ref[...], w_ref[LRU_WIDTH:, :], preferred_element_type=F32)
    o_ref[...] = x_ref[...] + acc


def _outproj(x2, ya, yb, w_out):
    n, d = x2.shape
    row = lambda i: (i, 0)
    return pl.pallas_call(
        _outproj_kernel,
        grid=(n // TM_OUT,),
        in_specs=[
            pl.BlockSpec((TM_OUT, d), row),
            pl.BlockSpec((TM_OUT, ya.shape[1]), row),
            pl.BlockSpec((TM_OUT, yb.shape[1]), row),
            pl.BlockSpec(w_out.shape, lambda i: (0, 0)),
        ],
        out_specs=pl.BlockSpec((TM_OUT, d), row),
        out_shape=jax.ShapeDtypeStruct((n, d), F32),
        compiler_params=pltpu.CompilerParams(
            dimension_semantics=("arbitrary",), vmem_limit_bytes=VMEM_LIMIT),
        name="outproj_residual",
    )(x2, ya, yb, w_out)


def _pack_rows(w):
    rows, cols = w.shape

    def pack_kernel(w_ref, o_ref):
        o_ref[...] = _packed(w_ref[...])

    return pl.pallas_call(
        pack_kernel,
        grid=(cols // PACK_COLS,),
        in_specs=[pl.BlockSpec((rows, PACK_COLS), lambda j: (0, j))],
        out_specs=pl.BlockSpec((rows // 2, PACK_COLS), lambda j: (0, j)),
        out_shape=jax.ShapeDtypeStruct((rows // 2, cols), jnp.uint32),
        name="pack_weight_rows",
    )(w)


def kernel(x, norm_gain, w_in, conv_w, conv_b, w_rg, b_rg, w_ig, b_ig, lru_lambda,
           q_norm_gain, k_norm_gain, lambda_q1, lambda_k1, lambda_q2, lambda_k2,
           subln_gain, w_out, rel_bias):
    bsz, seq, d = x.shape
    assert norm_gain.shape[0] == 1, "single layer only"
    assert seq % TM == 0 and seq % TA == 0 and (bsz * seq) % TM_OUT == 0
    n_groups = SEC // HEAD_DIM
    wg = (0.5 * jnp.concatenate([w_rg[0], w_ig[0]], axis=-1)).astype(BF16)
    bg = 0.5 * jnp.concatenate([b_rg[0], b_ig[0]], axis=-1)[:, None, :]
    gq = jnp.tile(q_norm_gain[0], n_groups)[None, :]
    gk = jnp.tile(k_norm_gain[0], n_groups)[None, :]
    gsum = jnp.asarray(np.kron(np.eye(256 // HEAD_DIM), np.ones((HEAD_DIM, HEAD_DIM))) / HEAD_DIM,
                       BF16)

    ylru, qt, k, vt, sg = _inproj(
        x, norm_gain, _pack_rows(w_in[0]), conv_w[0], conv_b, wg, bg, lru_lambda,
        gq, gk, gsum)
    yatt = _attention(rel_bias, lambda_q1, lambda_k1, lambda_q2, lambda_k2, subln_gain,
                      qt, k, vt, sg)
    out = _outproj(x.reshape(bsz * seq, d), ylru.reshape(bsz * seq, -1),
                   yatt.reshape(bsz * seq, -1), w_out[0].astype(BF16))
    return out.reshape(bsz, seq, d)
```

```python
import functools
import math

import jax
import jax.numpy as jnp
import numpy as np
from jax import lax
from jax.experimental import pallas as pl
from jax.experimental.pallas import tpu as pltpu

F32 = jnp.float32
BF16 = jnp.bfloat16

D_MODEL = 1024
LRU_WIDTH = 1024
LRU_BLOCKS = 8
LRU_BLOCK = LRU_WIDTH // LRU_BLOCKS
CONV_WIDTH = 4
LRU_C = 8.0
ATT_HEADS = 8
HEAD_DIM = 64
V_DIM = 2 * HEAD_DIM
SEC = 1024
N_BUCKETS = 32
MAX_DISTANCE = 128
EPS = 1e-6
NEG_INF = -1e30
TINY = 1e-30
LAM_INIT = 0.8 - 0.6 * math.exp(-0.3 * 0)
LOG2E = math.log2(math.e)

SUBLANES = 8
TM = 512
TM_OUT = 1024
SLAB = 256
PACK_COLS = 512
SUM_ROWS = 16
TA = 256
VMEM_LIMIT = 58 * 1024 * 1024


def _silu(x):
    hx = 0.5 * x
    return hx + hx * jnp.tanh(hx)


def _packed(x):
    return pltpu.bitcast(x.astype(BF16), jnp.uint32)


def _inproj_kernel(tiles_per_batch,
                   x_ref, ng_ref, w_ref, cw_ref, cb_ref, wg_ref, bg_ref, lam_ref,
                   gq_ref, gk_ref, gsum_ref,
                   ylru_ref, qt_ref, k_ref, vt_ref, sg_ref,
                   xbuf, a_buf, u_buf, hcar, zbuf):
    step = pl.program_id(0)

    @pl.when(step == 0)
    def _():
        zbuf[...] = jnp.zeros_like(zbuf)

    @pl.when(jnp.maximum(step - 1, 0) % tiles_per_batch == 0)
    def _():
        xbuf[0:SUBLANES, :] = jnp.zeros((SUBLANES, LRU_WIDTH), F32)
        hcar[...] = jnp.zeros_like(hcar)

    x = x_ref[0]
    ms = jnp.mean(x * x, axis=-1, keepdims=True)
    hb = (x * lax.rsqrt(ms + EPS) * ng_ref[...]).astype(BF16)

    def proj(c, s):
        col = c * SEC + s * SLAB
        prev = zbuf[:, col:col + SLAB]
        w_slab = pltpu.bitcast(w_ref[:, col:col + SLAB], BF16)
        zbuf[:, col:col + SLAB] = jnp.dot(hb, w_slab, preferred_element_type=F32)
        return prev

    def qk_norm(zc, gain):
        sq = (zc * zc).astype(BF16)
        mean_sq = jnp.dot(sq, gsum_ref[...], preferred_element_type=F32)
        return zc * lax.rsqrt(mean_sq + EPS) * gain

    z = -lam_ref[...]
    softplus = jnp.maximum(z, 0.0) + jnp.log1p(jnp.exp(-jnp.abs(z)))
    half_log2_a_max = (-0.5 * LRU_C * LOG2E) * softplus
    gq = gq_ref[...] * (LOG2E / math.sqrt(HEAD_DIM))
    row = lax.broadcasted_iota(jnp.int32, (SUBLANES, SLAB), 0)
    n_slabs = LRU_WIDTH // SLAB

    xl_next = proj(0, 0)
    for s in range(n_slabs):
        ss = slice(s * SLAB, (s + 1) * SLAB)
        xl = xl_next
        if s + 1 < n_slabs:
            xl_next = proj(0, s + 1)

        xbuf[SUBLANES:SUBLANES + TM, ss] = xl
        xc = cw_ref[3:4, ss] * xl + cb_ref[:, ss]
        for d in range(1, CONV_WIDTH):
            xc = xc + cw_ref[3 - d:4 - d, ss] * xbuf[SUBLANES - d:SUBLANES - d + TM, ss]
        xbuf[0:SUBLANES, ss] = xbuf[TM:TM + SUBLANES, ss]

        xcb = xc.astype(BF16)
        for j in range(SLAB // LRU_BLOCK):
            blk = s * (SLAB // LRU_BLOCK) + j
            cs = slice(blk * LRU_BLOCK, (blk + 1) * LRU_BLOCK)
            js = slice(j * LRU_BLOCK, (j + 1) * LRU_BLOCK)
            th = jnp.tanh(jnp.dot(xcb[:, js], wg_ref[blk], preferred_element_type=F32)
                          + bg_ref[blk])
            c2 = half_log2_a_max[:, cs]
            a = jnp.exp2(c2 * th[:, :LRU_BLOCK] + c2)
            hxc = 0.5 * xc[:, js]
            gated_x = hxc * th[:, LRU_BLOCK:] + hxc
            y = 1.0 - a * a
            a_buf[:, cs] = a
            u_buf[:, cs] = (y * lax.rsqrt(jnp.maximum(y, TINY))) * gated_x
            if j == 0:
                zq = proj(2, s)

        carry = hcar[:, ss]
        n_groups = TM // SUBLANES
        for g in range(n_groups):
            if g == 0:
                zk = proj(3, s)
            elif g == n_groups // 3:
                gl = proj(1, s)
            elif g == 2 * n_groups // 3:
                zv = proj(4, s)
            rs = slice(g * SUBLANES, (g + 1) * SUBLANES)
            a = a_buf[rs, ss]
            u = u_buf[rs, ss]
            for k in (1, 2, 4):
                keep = row >= k
                a_sh = jnp.where(keep, pltpu.roll(a, k, 0), 1.0)
                u_sh = jnp.where(keep, pltpu.roll(u, k, 0), 0.0)
                u = a * u_sh + u
                a = a * a_sh
            h = a * carry + u
            u_buf[rs, ss] = h
            carry = h[SUBLANES - 1:SUBLANES, :]
        hcar[:, ss] = carry

        zg = proj(5, s)
        qt_ref[0, s * SLAB // 2:(s + 1) * SLAB // 2, :] = _packed(qk_norm(zq, gq[:, ss]).T)
        k_ref[0, :, ss] = _packed(qk_norm(zk, gk_ref[:, ss]))
        ylru_ref[0, :, ss] = (u_buf[:, ss] * _silu(gl)).astype(BF16)
        a_buf[:, ss] = zv
        vt_ref[0, s * SLAB // 2:(s + 1) * SLAB // 2, :] = _packed(a_buf[:, ss].T)
        sg_ref[0, :, ss] = _silu(zg).astype(BF16)


def _inproj(x, ng, w_in, cw, cb, wg, bg, lam, gq, gk, gsum):
    bsz, seq, _ = x.shape
    d_in = w_in.shape[1]
    tpb = seq // TM
    n_tiles = bsz * tpb
    const2 = lambda n: (0, 0)
    const3 = lambda n: (0, 0, 0)

    def in_tile(n):
        m = jnp.minimum(n, n_tiles - 1)
        return (m // tpb, m % tpb, 0)

    def out_tile(n):
        m = jnp.maximum(n - 1, 0)
        return (m // tpb, m % tpb, 0)

    def out_tile_t(n):
        m = jnp.maximum(n - 1, 0)
        return (m // tpb, 0, m % tpb)

    tile = pl.BlockSpec((1, TM, SEC), out_tile)
    tile_p = pl.BlockSpec((1, TM // 2, SEC), out_tile)
    tile_tp = pl.BlockSpec((1, SEC // 2, TM), out_tile_t)
    sds = jax.ShapeDtypeStruct((bsz, seq, SEC), BF16)
    sds_p = jax.ShapeDtypeStruct((bsz, seq // 2, SEC), jnp.uint32)
    sds_tp = jax.ShapeDtypeStruct((bsz, SEC // 2, seq), jnp.uint32)
    return pl.pallas_call(
        functools.partial(_inproj_kernel, tpb),
        grid=(n_tiles + 1,),
        in_specs=[
            pl.BlockSpec((1, TM, D_MODEL), in_tile),
            pl.BlockSpec((1, D_MODEL), const2),
            pl.BlockSpec((D_MODEL // 2, d_in), const2, pipeline_mode=pl.Buffered(1)),
            pl.BlockSpec((CONV_WIDTH, LRU_WIDTH), const2),
            pl.BlockSpec((1, LRU_WIDTH), const2),
            pl.BlockSpec((LRU_BLOCKS, LRU_BLOCK, 2 * LRU_BLOCK), const3),
            pl.BlockSpec((LRU_BLOCKS, 1, 2 * LRU_BLOCK), const3),
            pl.BlockSpec((1, LRU_WIDTH), const2),
            pl.BlockSpec((1, SEC), const2),
            pl.BlockSpec((1, SEC), const2),
            pl.BlockSpec((256, 256), const2),
        ],
        out_specs=[tile, tile_tp, tile_p, tile_tp, tile],
        out_shape=[sds, sds_tp, sds_p, sds_tp, sds],
        scratch_shapes=[
            pltpu.VMEM((TM + SUBLANES, LRU_WIDTH), F32),
            pltpu.VMEM((TM, LRU_WIDTH), F32),
            pltpu.VMEM((TM, LRU_WIDTH), F32),
            pltpu.VMEM((1, LRU_WIDTH), F32),
            pltpu.VMEM((TM, d_in), F32),
        ],
        compiler_params=pltpu.CompilerParams(
            dimension_semantics=("arbitrary",),
            vmem_limit_bytes=VMEM_LIMIT),
        name="inproj_lru_qknorm",
    )(x, ng, w_in, cw, cb, wg, bg, lam, gq, gk, gsum)


def _bucket_table():
    rel = np.arange(TA)[None, :] - np.arange(2 * TA)[:, None] + TA
    n = np.maximum(rel, 0)
    max_exact = N_BUCKETS // 2
    nf = np.maximum(n, 1).astype(np.float32)
    large = max_exact + (np.log(nf / np.float32(max_exact)) / np.float32(math.log(MAX_DISTANCE / max_exact))
                         * np.float32(N_BUCKETS - max_exact)).astype(np.int32)
    large = np.minimum(large, N_BUCKETS - 1)
    bucket = np.where(n < max_exact, n, large)
    return np.where(rel >= 0, bucket, -1).astype(np.int32)


def _attn_kernel(rb_ref, bucket_ref, lq1_ref, lk1_ref, lq2_ref, lk2_ref, sgain_ref,
                 qt_ref, qn_ref, k_ref, vt_ref, sg_ref, o_ref,
                 w_ref, m_ref, acc_ref, sa_ref, sb_ref, ma_ref, mb_ref):
    b = pl.program_id(0)
    i = pl.program_id(1)

    @pl.when((b == 0) & (i == 0))
    def _():
        bk = bucket_ref[...]
        for h in range(ATT_HEADS):
            far = rb_ref[N_BUCKETS - 1, h]
            w = jnp.full(bk.shape, NEG_INF, F32)
            for kk in range(N_BUCKETS):
                w = jnp.where(bk == kk, (rb_ref[kk, h] - far) * LOG2E, w)
            w_ref[h] = w

    lam = (jnp.exp(jnp.sum(lq1_ref[...] * lk1_ref[...], axis=-1, keepdims=True))
           - jnp.exp(jnp.sum(lq2_ref[...] * lk2_ref[...], axis=-1, keepdims=True))
           + LAM_INIT)

    zeros_half = jnp.zeros((HEAD_DIM, TA), BF16)

    def query_operand(q_packed):
        qh = pltpu.bitcast(q_packed, BF16)
        return jnp.concatenate(
            [jnp.concatenate([qh[0:HEAD_DIM, :], zeros_half], axis=0),
             jnp.concatenate([zeros_half, qh[HEAD_DIM:V_DIM, :]], axis=0)], axis=1)

    head_rows = [slice(h * HEAD_DIM, (h + 1) * HEAD_DIM) for h in range(ATT_HEADS)]
    qq = [query_operand(qt_ref[0, head_rows[h], :]) for h in range(ATT_HEADS)]

    ones_rows = (lax.broadcasted_iota(jnp.int32, (SUM_ROWS, TA), 0) == 0).astype(BF16)

    m_ref[...] = jnp.full(m_ref.shape, NEG_INF, F32)
    acc_ref[...] = jnp.zeros(acc_ref.shape, F32)

    last_tile = 2 * k_ref.shape[1] // TA - 1

    def scores_head(h, t, buf, cmax, q_operand=None):
        k0 = pl.multiple_of(jnp.minimum(t, last_tile) * (TA // 2), TA // 2)
        kt = pltpu.bitcast(k_ref[0, pl.ds(k0, TA // 2), h * V_DIM:(h + 1) * V_DIM],
                           BF16)
        qop = qq[h] if q_operand is None else q_operand
        s = jnp.dot(kt, qop, preferred_element_type=F32)
        buf[h] = s.astype(BF16)
        cmax[h] = jnp.max(s, axis=0, keepdims=True)

    def add_bias(t, buf, cmax):
        @pl.when(t == i)
        def _():
            for h in range(ATT_HEADS):
                w = w_ref[h, TA:2 * TA, :]
                s = buf[h].astype(F32) + jnp.concatenate([w, w], axis=1)
                buf[h] = s.astype(BF16)
                cmax[h] = jnp.max(s, axis=0, keepdims=True)

        @pl.when(t == i - 1)
        def _():
            hq = TA // 2
            for h in range(ATT_HEADS):
                w = w_ref[h, hq:TA, 0:hq]
                for c0 in (0, TA):
                    lo = buf[h, hq:TA, c0:c0 + hq].astype(F32) + w
                    buf[h, hq:TA, c0:c0 + hq] = lo.astype(BF16)
                    top = jnp.max(buf[h, 0:hq, c0:c0 + hq], axis=0, keepdims=True).astype(F32)
                    cmax[h, :, c0:c0 + hq] = jnp.maximum(
                        top, jnp.max(lo, axis=0, keepdims=True))

    def softmax_pv_head(h, t, buf, cmax):
        k0 = pl.multiple_of(t * TA, TA)
        s = buf[h]
        m_prev = m_ref[h]
        m_new = jnp.maximum(m_prev, cmax[h]).astype(BF16)
        alpha = jnp.exp2(m_prev - m_new.astype(F32))
        p = jnp.exp2(s - m_new)
        vt = pltpu.bitcast(vt_ref[0, h * HEAD_DIM:(h + 1) * HEAD_DIM, pl.ds(k0, TA)],
                           BF16)
        vt = jnp.concatenate([vt, ones_rows], axis=0)
        acc_ref[h] = alpha * acc_ref[h] + jnp.dot(vt, p, preferred_element_type=F32)
        m_ref[h] = m_new.astype(F32)

    def stage(t_cur, cur, t_next, nxt):
        for h in range(ATT_HEADS):
            scores_head(h, t_next, *nxt)
            softmax_pv_head(h, t_cur, *cur)
        add_bias(t_next, *nxt)

    buf_a = (sa_ref, ma_ref)
    buf_b = (sb_ref, mb_ref)
    n_tiles = i + 1

    @pl.when(i == 0)
    def _():
        for h in range(ATT_HEADS):
            scores_head(h, 0, *buf_a)

    add_bias(0, *buf_a)

    def pair_body(u, c):
        t0 = 2 * u
        stage(t0, buf_a, t0 + 1, buf_b)
        stage(t0 + 1, buf_b, t0 + 2, buf_a)
        return c

    n_pairs = n_tiles // 2
    lax.fori_loop(0, n_pairs - 1, pair_body, 0)

    @pl.when(n_pairs >= 1)
    def _():
        t0 = 2 * (n_pairs - 1)
        stage(t0, buf_a, t0 + 1, buf_b)
        is_next = t0 + 2 >= n_tiles
        t_next = jnp.where(is_next, 0, t0 + 2)
        for h in range(ATT_HEADS):
            q_packed = jnp.where(is_next, qn_ref[0, head_rows[h], :], qt_ref[0, head_rows[h], :])
            scores_head(h, t_next, *buf_a, q_operand=query_operand(q_packed))
            softmax_pv_head(h, t0 + 1, *buf_b)
        add_bias(t0 + 2, *buf_a)

    @pl.when(n_tiles % 2 == 1)
    def _():
        for h in range(ATT_HEADS):
            softmax_pv_head(h, n_tiles - 1, *buf_a)
            scores_head(h, 0, *buf_a, q_operand=query_operand(qn_ref[0, head_rows[h], :]))

    gain = sgain_ref[...] * (1.0 - LAM_INIT)
    for h in range(ATT_HEADS):
        hs = slice(h * V_DIM, (h + 1) * V_DIM)
        o = acc_ref[h, 0:V_DIM, :] * (1.0 / acc_ref[h, V_DIM:V_DIM + 1, :])
        o = o[:, 0:TA] - lam * o[:, TA:2 * TA]
        ms = jnp.mean(o * o, axis=0, keepdims=True)
        y = (o * lax.rsqrt(ms + EPS)).T * gain
        o_ref[0, :, hs] = (y * sg_ref[0, :, hs].astype(F32)).astype(BF16)


def _attention(rel_bias, lq1, lk1, lq2, lk2, sgain, qt, k, vt, sg):
    bsz, seq, width = sg.shape
    table = _bucket_table()
    near = table[:TA] != N_BUCKETS - 1
    assert not near[:TA // 2].any() and not near[:, TA // 2:].any()
    bucket = jnp.asarray(table)
    const2 = lambda b, i: (0, 0)
    return pl.pallas_call(
        _attn_kernel,
        grid=(bsz, seq // TA),
        in_specs=[
            pl.BlockSpec(memory_space=pltpu.SMEM),
            pl.BlockSpec((2 * TA, TA), const2),
            pl.BlockSpec((1, HEAD_DIM), const2),
            pl.BlockSpec((1, HEAD_DIM), const2),
            pl.BlockSpec((1, HEAD_DIM), const2),
            pl.BlockSpec((1, HEAD_DIM), const2),
            pl.BlockSpec((1, V_DIM), const2),
            pl.BlockSpec((1, width // 2, TA), lambda b, i: (b, 0, i)),
            pl.BlockSpec((1, width // 2, TA),
                         lambda b, i: (b, 0, jnp.minimum(i + 1, seq // TA - 1))),
            pl.BlockSpec((1, seq // 2, width), lambda b, i: (b, 0, 0)),
            pl.BlockSpec((1, width // 2, seq), lambda b, i: (b, 0, 0)),
            pl.BlockSpec((1, TA, width), lambda b, i: (b, i, 0)),
        ],
        out_specs=pl.BlockSpec((1, TA, width), lambda b, i: (b, i, 0)),
        out_shape=jax.ShapeDtypeStruct((bsz, seq, width), BF16),
        scratch_shapes=[
            pltpu.VMEM((ATT_HEADS, 2 * TA, TA), F32),
            pltpu.VMEM((ATT_HEADS, 1, 2 * TA), F32),
            pltpu.VMEM((ATT_HEADS, V_DIM + SUM_ROWS, 2 * TA), F32),
            pltpu.VMEM((ATT_HEADS, TA, 2 * TA), BF16),
            pltpu.VMEM((ATT_HEADS, TA, 2 * TA), BF16),
            pltpu.VMEM((ATT_HEADS, 1, 2 * TA), F32),
            pltpu.VMEM((ATT_HEADS, 1, 2 * TA), F32),
        ],
        compiler_params=pltpu.CompilerParams(
            dimension_semantics=("arbitrary", "arbitrary"),
            vmem_limit_bytes=VMEM_LIMIT),
        name="diff_attention",
    )(rel_bias, bucket, lq1, lk1, lq2, lk2, sgain, qt, qt, k, vt, sg)


def _outproj_kernel(x_ref, ya_ref, yb_ref, w_ref, o_ref):
    acc = jnp.dot(ya_ref[...], w_ref[0:LRU_WIDTH, :], preferred_element_type=F32)
    acc = acc + jnp.dot(yb_ref[...], w_ref[LRU_WIDTH:, :], preferred_element_type=F32)
    o_ref[...] = x_ref[...] + acc


def _outproj(x2, ya, yb, w_out):
    n, d = x2.shape
    row = lambda i: (i, 0)
    return pl.pallas_call(
        _outproj_kernel,
        grid=(n // TM_OUT,),
        in_specs=[
            pl.BlockSpec((TM_OUT, d), row),
            pl.BlockSpec((TM_OUT, ya.shape[1]), row),
            pl.BlockSpec((TM_OUT, yb.shape[1]), row),
            pl.BlockSpec(w_out.shape, lambda i: (0, 0)),
        ],
        out_specs=pl.BlockSpec((TM_OUT, d), row),
        out_shape=jax.ShapeDtypeStruct((n, d), F32),
        compiler_params=pltpu.CompilerParams(
            dimension_semantics=("arbitrary",), vmem_limit_bytes=VMEM_LIMIT),
        name="outproj_residual",
    )(x2, ya, yb, w_out)


def _pack_rows(w):
    rows, cols = w.shape

    def pack_kernel(w_ref, o_ref):
        o_ref[...] = _packed(w_ref[...])

    return pl.pallas_call(
        pack_kernel,
        grid=(cols // PACK_COLS,),
        in_specs=[pl.BlockSpec((rows, PACK_COLS), lambda j: (0, j))],
        out_specs=pl.BlockSpec((rows // 2, PACK_COLS), lambda j: (0, j)),
        out_shape=jax.ShapeDtypeStruct((rows // 2, cols), jnp.uint32),
        name="pack_weight_rows",
    )(w)


def kernel(x, norm_gain, w_in, conv_w, conv_b, w_rg, b_rg, w_ig, b_ig, lru_lambda,
           q_norm_gain, k_norm_gain, lambda_q1, lambda_k1, lambda_q2, lambda_k2,
           subln_gain, w_out, rel_bias):
    bsz, seq, d = x.shape
    assert norm_gain.shape[0] == 1, "single layer only"
    assert seq % TM == 0 and seq % TA == 0 and (bsz * seq) % TM_OUT == 0
    n_groups = SEC // HEAD_DIM
    wg = (0.5 * jnp.concatenate([w_rg[0], w_ig[0]], axis=-1)).astype(BF16)
    bg = 0.5 * jnp.concatenate([b_rg[0], b_ig[0]], axis=-1)[:, None, :]
    gq = jnp.tile(q_norm_gain[0], n_groups)[None, :]
    gk = jnp.tile(k_norm_gain[0], n_groups)[None, :]
    gsum = jnp.asarray(np.kron(np.eye(256 // HEAD_DIM), np.ones((HEAD_DIM, HEAD_DIM))) / HEAD_DIM,
                       BF16)

    ylru, qt, k, vt, sg = _inproj(
        x, norm_gain, _pack_rows(w_in[0]), conv_w[0], conv_b, wg, bg, lru_lambda,
        gq, gk, gsum)
    yatt = _attention(rel_bias, lambda_q1, lambda_k1, lambda_q2, lambda_k2, subln_gain,
                      qt, k, vt, sg)
    out = _outproj(x.reshape(bsz * seq, d), ylru.reshape(bsz * seq, -1),
                   yatt.reshape(bsz * seq, -1), w_out[0].astype(BF16))
    return out.reshape(bsz, seq, d)
```

```python
import functools
import math

import jax
import jax.numpy as jnp
import numpy as np
from jax import lax
from jax.experimental import pallas as pl
from jax.experimental.pallas import tpu as pltpu

F32 = jnp.float32
BF16 = jnp.bfloat16

D_MODEL = 1024
LRU_WIDTH = 1024
LRU_BLOCKS = 8
LRU_BLOCK = LRU_WIDTH // LRU_BLOCKS
CONV_WIDTH = 4
LRU_C = 8.0
ATT_HEADS = 8
HEAD_DIM = 64
V_DIM = 2 * HEAD_DIM
SEC = 1024
N_BUCKETS = 32
MAX_DISTANCE = 128
EPS = 1e-6
NEG_INF = -1e30
TINY = 1e-30
LAM_INIT = 0.8 - 0.6 * math.exp(-0.3 * 0)
LOG2E = math.log2(math.e)

SUBLANES = 8
TM = 512
SLAB = 256
PACK_COLS = 512
TA = 256
SUM_ROWS = 16
VMEM_LIMIT = 58 * 1024 * 1024


def _silu(x):
    hx = 0.5 * x
    return hx + hx * jnp.tanh(hx)


def _packed(x):
    return pltpu.bitcast(x.astype(BF16), jnp.uint32)


def _inproj_kernel(tiles_per_batch,
                   x_ref, ng_ref, w_ref, cw_ref, cb_ref, wg_ref, bg_ref, lam_ref,
                   gq_ref, gk_ref, gsum_ref,
                   ylru_ref, qt_ref, k_ref, vt_ref, sg_ref,
                   xbuf, a_buf, u_buf, hcar, zbuf):
    step = pl.program_id(0)

    @pl.when(step == 0)
    def _():
        zbuf[...] = jnp.zeros_like(zbuf)

    @pl.when(jnp.maximum(step - 1, 0) % tiles_per_batch == 0)
    def _():
        xbuf[0:SUBLANES, :] = jnp.zeros((SUBLANES, LRU_WIDTH), F32)
        hcar[...] = jnp.zeros_like(hcar)

    x = x_ref[0]
    ms = jnp.mean(x * x, axis=-1, keepdims=True)
    hb = (x * lax.rsqrt(ms + EPS) * ng_ref[...]).astype(BF16)

    def proj(c, s):
        col = c * SEC + s * SLAB
        prev = zbuf[:, col:col + SLAB]
        w_slab = pltpu.bitcast(w_ref[:, col:col + SLAB], BF16)
        zbuf[:, col:col + SLAB] = jnp.dot(hb, w_slab, preferred_element_type=F32)
        return prev

    def qk_norm(zc, gain):
        sq = (zc * zc).astype(BF16)
        mean_sq = jnp.dot(sq, gsum_ref[...], preferred_element_type=F32)
        return zc * lax.rsqrt(mean_sq + EPS) * gain

    z = -lam_ref[...]
    softplus = jnp.maximum(z, 0.0) + jnp.log1p(jnp.exp(-jnp.abs(z)))
    half_log2_a_max = (-0.5 * LRU_C * LOG2E) * softplus
    gq = gq_ref[...] * (LOG2E / math.sqrt(HEAD_DIM))
    row = lax.broadcasted_iota(jnp.int32, (SUBLANES, SLAB), 0)
    n_slabs = LRU_WIDTH // SLAB

    xl_next = proj(0, 0)
    for s in range(n_slabs):
        ss = slice(s * SLAB, (s + 1) * SLAB)
        xl = xl_next
        if s + 1 < n_slabs:
            xl_next = proj(0, s + 1)

        xbuf[SUBLANES:SUBLANES + TM, ss] = xl
        xc = cw_ref[3:4, ss] * xl + cb_ref[:, ss]
        for d in range(1, CONV_WIDTH):
            xc = xc + cw_ref[3 - d:4 - d, ss] * xbuf[SUBLANES - d:SUBLANES - d + TM, ss]
        xbuf[0:SUBLANES, ss] = xbuf[TM:TM + SUBLANES, ss]

        xcb = xc.astype(BF16)
        for j in range(SLAB // LRU_BLOCK):
            blk = s * (SLAB // LRU_BLOCK) + j
            cs = slice(blk * LRU_BLOCK, (blk + 1) * LRU_BLOCK)
            js = slice(j * LRU_BLOCK, (j + 1) * LRU_BLOCK)
            th = jnp.tanh(jnp.dot(xcb[:, js], wg_ref[blk], preferred_element_type=F32)
                          + bg_ref[blk])
            c2 = half_log2_a_max[:, cs]
            a = jnp.exp2(c2 * th[:, :LRU_BLOCK] + c2)
            hxc = 0.5 * xc[:, js]
            gated_x = hxc * th[:, LRU_BLOCK:] + hxc
            y = 1.0 - a * a
            a_buf[:, cs] = a
            u_buf[:, cs] = (y * lax.rsqrt(jnp.maximum(y, TINY))) * gated_x
            if j == 0:
                zq = proj(2, s)

        carry = hcar[:, ss]
        n_groups = TM // SUBLANES
        for g in range(n_groups):
            if g == 0:
                zk = proj(3, s)
            elif g == n_groups // 3:
                gl = proj(1, s)
            elif g == 2 * n_groups // 3:
                zv = proj(4, s)
            rs = slice(g * SUBLANES, (g + 1) * SUBLANES)
            a = a_buf[rs, ss]
            u = u_buf[rs, ss]
            for k in (1, 2, 4):
                keep = row >= k
                a_sh = jnp.where(keep, pltpu.roll(a, k, 0), 1.0)
                u_sh = jnp.where(keep, pltpu.roll(u, k, 0), 0.0)
                u = a * u_sh + u
                a = a * a_sh
            h = a * carry + u
            u_buf[rs, ss] = h
            carry = h[SUBLANES - 1:SUBLANES, :]
        hcar[:, ss] = carry

        zg = proj(5, s)
        qt_ref[0, s * SLAB // 2:(s + 1) * SLAB // 2, :] = _packed(qk_norm(zq, gq[:, ss]).T)
        k_ref[0, :, ss] = _packed(qk_norm(zk, gk_ref[:, ss]))
        ylru_ref[0, :, ss] = (u_buf[:, ss] * _silu(gl)).astype(BF16)
        a_buf[:, ss] = zv
        vt_ref[0, s * SLAB // 2:(s + 1) * SLAB // 2, :] = _packed(a_buf[:, ss].T)
        sg_ref[0, :, ss] = _silu(zg).astype(BF16)


def _inproj(x, ng, w_in, cw, cb, wg, bg, lam, gq, gk, gsum):
    bsz, seq, _ = x.shape
    d_in = w_in.shape[1]
    tpb = seq // TM
    n_tiles = bsz * tpb
    const2 = lambda n: (0, 0)
    const3 = lambda n: (0, 0, 0)

    def in_tile(n):
        m = jnp.minimum(n, n_tiles - 1)
        return (m // tpb, m % tpb, 0)

    def out_tile(n):
        m = jnp.maximum(n - 1, 0)
        return (m // tpb, m % tpb, 0)

    def out_tile_t(n):
        m = jnp.maximum(n - 1, 0)
        return (m // tpb, 0, m % tpb)

    tile = pl.BlockSpec((1, TM, SEC), out_tile)
    tile_p = pl.BlockSpec((1, TM // 2, SEC), out_tile)
    tile_tp = pl.BlockSpec((1, SEC // 2, TM), out_tile_t)
    sds = jax.ShapeDtypeStruct((bsz, seq, SEC), BF16)
    sds_p = jax.ShapeDtypeStruct((bsz, seq // 2, SEC), jnp.uint32)
    sds_tp = jax.ShapeDtypeStruct((bsz, SEC // 2, seq), jnp.uint32)
    return pl.pallas_call(
        functools.partial(_inproj_kernel, tpb),
        grid=(n_tiles + 1,),
        in_specs=[
            pl.BlockSpec((1, TM, D_MODEL), in_tile),
            pl.BlockSpec((1, D_MODEL), const2),
            pl.BlockSpec((D_MODEL // 2, d_in), const2, pipeline_mode=pl.Buffered(1)),
            pl.BlockSpec((CONV_WIDTH, LRU_WIDTH), const2),
            pl.BlockSpec((1, LRU_WIDTH), const2),
            pl.BlockSpec((LRU_BLOCKS, LRU_BLOCK, 2 * LRU_BLOCK), const3),
            pl.BlockSpec((LRU_BLOCKS, 1, 2 * LRU_BLOCK), const3),
            pl.BlockSpec((1, LRU_WIDTH), const2),
            pl.BlockSpec((1, SEC), const2),
            pl.BlockSpec((1, SEC), const2),
            pl.BlockSpec((256, 256), const2),
        ],
        out_specs=[tile, tile_tp, tile_p, tile_tp, tile],
        out_shape=[sds, sds_tp, sds_p, sds_tp, sds],
        scratch_shapes=[
            pltpu.VMEM((TM + SUBLANES, LRU_WIDTH), F32),
            pltpu.VMEM((TM, LRU_WIDTH), F32),
            pltpu.VMEM((TM, LRU_WIDTH), F32),
            pltpu.VMEM((1, LRU_WIDTH), F32),
            pltpu.VMEM((TM, d_in), F32),
        ],
        compiler_params=pltpu.CompilerParams(
            dimension_semantics=("arbitrary",),
            vmem_limit_bytes=VMEM_LIMIT),
        name="inproj_lru_qknorm",
    )(x, ng, w_in, cw, cb, wg, bg, lam, gq, gk, gsum)


def _bucket_table():
    rel = np.arange(TA)[None, :] - np.arange(2 * TA)[:, None] + TA
    n = np.maximum(rel, 0)
    max_exact = N_BUCKETS // 2
    nf = np.maximum(n, 1).astype(np.float32)
    large = max_exact + (np.log(nf / np.float32(max_exact)) / np.float32(math.log(MAX_DISTANCE / max_exact))
                         * np.float32(N_BUCKETS - max_exact)).astype(np.int32)
    large = np.minimum(large, N_BUCKETS - 1)
    bucket = np.where(n < max_exact, n, large)
    return np.where(rel >= 0, bucket, -1).astype(np.int32)


def _attn_kernel(rb_ref, bucket_ref, lq1_ref, lk1_ref, lq2_ref, lk2_ref, sgain_ref,
                 qt_ref, qn_ref, k_ref, vt_ref, sg_ref, x_ref, ylru_ref, wo_ref, o_ref,
                 w_ref, m_ref, acc_ref, sa_ref, sb_ref, ma_ref, mb_ref, yatt_ref):
    b = pl.program_id(0)
    i = pl.program_id(1)

    @pl.when((b == 0) & (i == 0))
    def _():
        bk = bucket_ref[...]
        for h in range(ATT_HEADS):
            far = rb_ref[N_BUCKETS - 1, h]
            w = jnp.full(bk.shape, NEG_INF, F32)
            for kk in range(N_BUCKETS):
                w = jnp.where(bk == kk, (rb_ref[kk, h] - far) * LOG2E, w)
            w_ref[h] = w

    lam = (jnp.exp(jnp.sum(lq1_ref[...] * lk1_ref[...], axis=-1, keepdims=True))
           - jnp.exp(jnp.sum(lq2_ref[...] * lk2_ref[...], axis=-1, keepdims=True))
           + LAM_INIT)

    zeros_half = jnp.zeros((HEAD_DIM, TA), BF16)

    def query_operand(q_packed):
        qh = pltpu.bitcast(q_packed, BF16)
        return jnp.concatenate(
            [jnp.concatenate([qh[0:HEAD_DIM, :], zeros_half], axis=0),
             jnp.concatenate([zeros_half, qh[HEAD_DIM:V_DIM, :]], axis=0)], axis=1)

    head_rows = [slice(h * HEAD_DIM, (h + 1) * HEAD_DIM) for h in range(ATT_HEADS)]
    qq = [query_operand(qt_ref[0, head_rows[h], :]) for h in range(ATT_HEADS)]

    ones_rows = (lax.broadcasted_iota(jnp.int32, (SUM_ROWS, TA), 0) == 0).astype(BF16)

    m_ref[...] = jnp.full(m_ref.shape, NEG_INF, F32)
    acc_ref[...] = jnp.zeros(acc_ref.shape, F32)

    last_tile = 2 * k_ref.shape[1] // TA - 1

    def scores_head(h, t, buf, cmax, q_operand=None):
        k0 = pl.multiple_of(jnp.minimum(t, last_tile) * (TA // 2), TA // 2)
        kt = pltpu.bitcast(k_ref[0, pl.ds(k0, TA // 2), h * V_DIM:(h + 1) * V_DIM],
                           BF16)
        qop = qq[h] if q_operand is None else q_operand
        s = jnp.dot(kt, qop, preferred_element_type=F32)
        buf[h] = s
        cmax[h] = jnp.max(s, axis=0, keepdims=True)

    def add_bias(t, buf, cmax):
        @pl.when(t == i)
        def _():
            for h in range(ATT_HEADS):
                w = w_ref[h, TA:2 * TA, :]
                s = buf[h] + jnp.concatenate([w, w], axis=1)
                buf[h] = s
                cmax[h] = jnp.max(s, axis=0, keepdims=True)

        @pl.when(t == i - 1)
        def _():
            hq = TA // 2
            for h in range(ATT_HEADS):
                w = w_ref[h, hq:TA, 0:hq]
                for c0 in (0, TA):
                    lo = buf[h, hq:TA, c0:c0 + hq] + w
                    buf[h, hq:TA, c0:c0 + hq] = lo
                    top = jnp.max(buf[h, 0:hq, c0:c0 + hq], axis=0, keepdims=True)
                    cmax[h, :, c0:c0 + hq] = jnp.maximum(
                        top, jnp.max(lo, axis=0, keepdims=True))

    def softmax_pv_head(h, t, buf, cmax):
        k0 = pl.multiple_of(t * TA, TA)
        s = buf[h]
        m_prev = m_ref[h]
        m_new = jnp.maximum(m_prev, cmax[h])
        alpha = jnp.exp2(m_prev - m_new)
        p = jnp.exp2(s - m_new).astype(BF16)
        vt = pltpu.bitcast(vt_ref[0, h * HEAD_DIM:(h + 1) * HEAD_DIM, pl.ds(k0, TA)],
                           BF16)
        vt = jnp.concatenate([vt, ones_rows], axis=0)
        acc_ref[h] = alpha * acc_ref[h] + jnp.dot(vt, p, preferred_element_type=F32)
        m_ref[h] = m_new

    def stage(t_cur, cur, t_next, nxt):
        for h in range(ATT_HEADS):
            scores_head(h, t_next, *nxt)
            softmax_pv_head(h, t_cur, *cur)
        add_bias(t_next, *nxt)

    buf_a = (sa_ref, ma_ref)
    buf_b = (sb_ref, mb_ref)
    n_tiles = i + 1

    @pl.when(i == 0)
    def _():
        for h in range(ATT_HEADS):
            scores_head(h, 0, *buf_a)

    add_bias(0, *buf_a)

    def pair_body(u, c):
        t0 = 2 * u
        stage(t0, buf_a, t0 + 1, buf_b)
        stage(t0 + 1, buf_b, t0 + 2, buf_a)
        return c

    n_pairs = n_tiles // 2
    lax.fori_loop(0, n_pairs - 1, pair_body, 0)

    @pl.when(n_pairs >= 1)
    def _():
        t0 = 2 * (n_pairs - 1)
        stage(t0, buf_a, t0 + 1, buf_b)
        is_next = t0 + 2 >= n_tiles
        t_next = jnp.where(is_next, 0, t0 + 2)
        for h in range(ATT_HEADS):
            q_packed = jnp.where(is_next, qn_ref[0, head_rows[h], :], qt_ref[0, head_rows[h], :])
            scores_head(h, t_next, *buf_a, q_operand=query_operand(q_packed))
            softmax_pv_head(h, t0 + 1, *buf_b)
        add_bias(t0 + 2, *buf_a)

    @pl.when(n_tiles % 2 == 1)
    def _():
        for h in range(ATT_HEADS):
            softmax_pv_head(h, n_tiles - 1, *buf_a)
            scores_head(h, 0, *buf_a, q_operand=query_operand(qn_ref[0, head_rows[h], :]))

    half = wo_ref.shape[0] // 2
    out = x_ref[0] + jnp.dot(ylru_ref[0], pltpu.bitcast(wo_ref[0:half, :], BF16),
                             preferred_element_type=F32)
    gain = sgain_ref[...] * (1.0 - LAM_INIT)
    for h in range(ATT_HEADS):
        hs = slice(h * V_DIM, (h + 1) * V_DIM)
        o = acc_ref[h, 0:V_DIM, :] * (1.0 / acc_ref[h, V_DIM:V_DIM + 1, :])
        o = o[:, 0:TA] - lam * o[:, TA:2 * TA]
        ms = jnp.mean(o * o, axis=0, keepdims=True)
        y = (o * lax.rsqrt(ms + EPS)).T * gain
        yatt_ref[:, hs] = (y * sg_ref[0, :, hs].astype(F32)).astype(BF16)
    o_ref[0] = out + jnp.dot(yatt_ref[...], pltpu.bitcast(wo_ref[half:, :], BF16),
                             preferred_element_type=F32)


def _attention(rel_bias, lq1, lk1, lq2, lk2, sgain, qt, k, vt, sg, x, ylru, w_out):
    bsz, seq, width = sg.shape
    table = _bucket_table()
    near = table[:TA] != N_BUCKETS - 1
    assert not near[:TA // 2].any() and not near[:, TA // 2:].any()
    bucket = jnp.asarray(table)
    const2 = lambda b, i: (0, 0)
    return pl.pallas_call(
        _attn_kernel,
        grid=(bsz, seq // TA),
        in_specs=[
            pl.BlockSpec(memory_space=pltpu.SMEM),
            pl.BlockSpec((2 * TA, TA), const2),
            pl.BlockSpec((1, HEAD_DIM), const2),
            pl.BlockSpec((1, HEAD_DIM), const2),
            pl.BlockSpec((1, HEAD_DIM), const2),
            pl.BlockSpec((1, HEAD_DIM), const2),
            pl.BlockSpec((1, V_DIM), const2),
            pl.BlockSpec((1, width // 2, TA), lambda b, i: (b, 0, i)),
            pl.BlockSpec((1, width // 2, TA),
                         lambda b, i: (b, 0, jnp.minimum(i + 1, seq // TA - 1))),
            pl.BlockSpec((1, seq // 2, width), lambda b, i: (b, 0, 0)),
            pl.BlockSpec((1, width // 2, seq), lambda b, i: (b, 0, 0),
                         pipeline_mode=pl.Buffered(1)),
            pl.BlockSpec((1, TA, width), lambda b, i: (b, i, 0)),
            pl.BlockSpec((1, TA, x.shape[2]), lambda b, i: (b, i, 0)),
            pl.BlockSpec((1, TA, ylru.shape[2]), lambda b, i: (b, i, 0)),
            pl.BlockSpec(w_out.shape, const2, pipeline_mode=pl.Buffered(1)),
        ],
        out_specs=pl.BlockSpec((1, TA, x.shape[2]), lambda b, i: (b, i, 0)),
        out_shape=jax.ShapeDtypeStruct(x.shape, F32),
        scratch_shapes=[
            pltpu.VMEM((ATT_HEADS, 2 * TA, TA), F32),
            pltpu.VMEM((ATT_HEADS, 1, 2 * TA), F32),
            pltpu.VMEM((ATT_HEADS, V_DIM + SUM_ROWS, 2 * TA), F32),
            pltpu.VMEM((ATT_HEADS, TA, 2 * TA), F32),
            pltpu.VMEM((ATT_HEADS, TA, 2 * TA), F32),
            pltpu.VMEM((ATT_HEADS, 1, 2 * TA), F32),
            pltpu.VMEM((ATT_HEADS, 1, 2 * TA), F32),
            pltpu.VMEM((TA, ATT_HEADS * V_DIM), BF16),
        ],
        compiler_params=pltpu.CompilerParams(
            dimension_semantics=("arbitrary", "arbitrary"),
            vmem_limit_bytes=VMEM_LIMIT),
        name="diff_attention",
    )(rel_bias, bucket, lq1, lk1, lq2, lk2, sgain, qt, qt, k, vt, sg, x, ylru, w_out)


def _pack_rows(w):
    rows, cols = w.shape

    def pack_kernel(w_ref, o_ref):
        o_ref[...] = _packed(w_ref[...])

    return pl.pallas_call(
        pack_kernel,
        grid=(cols // PACK_COLS,),
        in_specs=[pl.BlockSpec((rows, PACK_COLS), lambda j: (0, j))],
        out_specs=pl.BlockSpec((rows // 2, PACK_COLS), lambda j: (0, j)),
        out_shape=jax.ShapeDtypeStruct((rows // 2, cols), jnp.uint32),
        name="pack_weight_rows",
    )(w)


def kernel(x, norm_gain, w_in, conv_w, conv_b, w_rg, b_rg, w_ig, b_ig, lru_lambda,
           q_norm_gain, k_norm_gain, lambda_q1, lambda_k1, lambda_q2, lambda_k2,
           subln_gain, w_out, rel_bias):
    bsz, seq, d = x.shape
    assert norm_gain.shape[0] == 1, "single layer only"
    assert seq % TM == 0 and seq % TA == 0
    n_groups = SEC // HEAD_DIM
    wg = (0.5 * jnp.concatenate([w_rg[0], w_ig[0]], axis=-1)).astype(BF16)
    bg = 0.5 * jnp.concatenate([b_rg[0], b_ig[0]], axis=-1)[:, None, :]
    gq = jnp.tile(q_norm_gain[0], n_groups)[None, :]
    gk = jnp.tile(k_norm_gain[0], n_groups)[None, :]
    gsum = jnp.asarray(np.kron(np.eye(256 // HEAD_DIM), np.ones((HEAD_DIM, HEAD_DIM))) / HEAD_DIM,
                       BF16)

    ylru, qt, k, vt, sg = _inproj(
        x, norm_gain, _pack_rows(w_in[0]), conv_w[0], conv_b, wg, bg, lru_lambda,
        gq, gk, gsum)
    return _attention(rel_bias, lambda_q1, lambda_k1, lambda_q2, lambda_k2, subln_gain,
                      qt, k, vt, sg, x, ylru, _pack_rows(w_out[0]))
```

```python
import functools
import math

import jax
import jax.numpy as jnp
import numpy as np
from jax import lax
from jax.experimental import pallas as pl
from jax.experimental.pallas import tpu as pltpu

F32 = jnp.float32
BF16 = jnp.bfloat16

D_MODEL = 1024
LRU_WIDTH = 1024
LRU_BLOCKS = 8
LRU_BLOCK = LRU_WIDTH // LRU_BLOCKS
CONV_WIDTH = 4
LRU_C = 8.0
ATT_HEADS = 8
HEAD_DIM = 64
V_DIM = 2 * HEAD_DIM
SEC = 1024
N_BUCKETS = 32
MAX_DISTANCE = 128
EPS = 1e-6
NEG_INF = -1e30
TINY = 1e-30
LAM_INIT = 0.8 - 0.6 * math.exp(-0.3 * 0)
LOG2E = math.log2(math.e)

SUBLANES = 8
TM = 512
TM_OUT = 1024
SLAB = 256
N_SECTIONS = 6
TA = 256
SUM_ROWS = 16
VMEM_LIMIT = 58 * 1024 * 1024


def _silu(x):
    hx = 0.5 * x
    return hx + hx * jnp.tanh(hx)


def _packed(x):
    return pltpu.bitcast(x.astype(BF16), jnp.uint32)


def _inproj_kernel(tiles_per_batch,
                   x_ref, ng_ref, w_ref, cw_ref, cb_ref, wg_ref, bg_ref, lam_ref,
                   gq_ref, gk_ref, gsum_ref,
                   ylru_ref, qt_ref, k_ref, vt_ref, sg_ref,
                   xbuf, a_buf, u_buf, hcar, zbuf):
    step = pl.program_id(0)

    @pl.when(step == 0)
    def _():
        zbuf[...] = jnp.zeros_like(zbuf)

    @pl.when(jnp.maximum(step - 1, 0) % tiles_per_batch == 0)
    def _():
        xbuf[0:SUBLANES, :] = jnp.zeros((SUBLANES, LRU_WIDTH), F32)
        hcar[...] = jnp.zeros_like(hcar)

    x = x_ref[0]
    ms = jnp.mean(x * x, axis=-1, keepdims=True)
    hb = (x * lax.rsqrt(ms + EPS) * ng_ref[...]).astype(BF16)

    def proj_pair(p, s):
        col = (s * N_SECTIONS + 2 * p) * SLAB
        prev = zbuf[:, col:col + 2 * SLAB]
        w_slab = pltpu.bitcast(w_ref[:, col:col + 2 * SLAB], BF16)
        zbuf[:, col:col + 2 * SLAB] = jnp.dot(hb, w_slab, preferred_element_type=F32)
        return prev[:, :SLAB], prev[:, SLAB:]

    def qk_norm(zc, gain):
        sq = (zc * zc).astype(BF16)
        mean_sq = jnp.dot(sq, gsum_ref[...], preferred_element_type=F32)
        return zc * lax.rsqrt(mean_sq + EPS) * gain

    z = -lam_ref[...]
    softplus = jnp.maximum(z, 0.0) + jnp.log1p(jnp.exp(-jnp.abs(z)))
    half_log2_a_max = (-0.5 * LRU_C * LOG2E) * softplus
    gq = gq_ref[...] * (LOG2E / math.sqrt(HEAD_DIM))
    row = lax.broadcasted_iota(jnp.int32, (SUBLANES, SLAB), 0)
    n_slabs = LRU_WIDTH // SLAB

    for s in range(n_slabs):
        ss = slice(s * SLAB, (s + 1) * SLAB)
        xl, gl = proj_pair(0, s)

        xbuf[SUBLANES:SUBLANES + TM, ss] = xl
        xc = cw_ref[3:4, ss] * xl + cb_ref[:, ss]
        for d in range(1, CONV_WIDTH):
            xc = xc + cw_ref[3 - d:4 - d, ss] * xbuf[SUBLANES - d:SUBLANES - d + TM, ss]
        xbuf[0:SUBLANES, ss] = xbuf[TM:TM + SUBLANES, ss]

        xcb = xc.astype(BF16)
        for j in range(SLAB // LRU_BLOCK):
            blk = s * (SLAB // LRU_BLOCK) + j
            cs = slice(blk * LRU_BLOCK, (blk + 1) * LRU_BLOCK)
            js = slice(j * LRU_BLOCK, (j + 1) * LRU_BLOCK)
            th = jnp.tanh(jnp.dot(xcb[:, js], wg_ref[blk], preferred_element_type=F32)
                          + bg_ref[blk])
            c2 = half_log2_a_max[:, cs]
            a = jnp.exp2(c2 * th[:, :LRU_BLOCK] + c2)
            hxc = 0.5 * xc[:, js]
            gated_x = hxc * th[:, LRU_BLOCK:] + hxc
            y = 1.0 - a * a
            a_buf[:, cs] = a
            u_buf[:, cs] = (y * lax.rsqrt(jnp.maximum(y, TINY))) * gated_x
            if j == 0:
                zq, zk = proj_pair(1, s)

        carry = hcar[:, ss]
        n_groups = TM // SUBLANES
        for g in range(n_groups):
            if g == n_groups // 2:
                zv, zg = proj_pair(2, s)
            rs = slice(g * SUBLANES, (g + 1) * SUBLANES)
            a = a_buf[rs, ss]
            u = u_buf[rs, ss]
            for k in (1, 2, 4):
                keep = row >= k
                a_sh = jnp.where(keep, pltpu.roll(a, k, 0), 1.0)
                u_sh = jnp.where(keep, pltpu.roll(u, k, 0), 0.0)
                u = a * u_sh + u
                a = a * a_sh
            h = a * carry + u
            u_buf[rs, ss] = h
            carry = h[SUBLANES - 1:SUBLANES, :]
        hcar[:, ss] = carry

        qt_ref[0, s * SLAB // 2:(s + 1) * SLAB // 2, :] = _packed(qk_norm(zq, gq[:, ss]).T)
        k_ref[0, :, ss] = _packed(qk_norm(zk, gk_ref[:, ss]))
        ylru_ref[0, :, ss] = (u_buf[:, ss] * _silu(gl)).astype(BF16)
        a_buf[:, ss] = zv
        vt_ref[0, s * SLAB // 2:(s + 1) * SLAB // 2, :] = _packed(a_buf[:, ss].T)
        sg_ref[0, :, ss] = _silu(zg).astype(BF16)


def _inproj(x, ng, w_in, cw, cb, wg, bg, lam, gq, gk, gsum):
    bsz, seq, _ = x.shape
    d_in = w_in.shape[1]
    tpb = seq // TM
    n_tiles = bsz * tpb
    const2 = lambda n: (0, 0)
    const3 = lambda n: (0, 0, 0)

    def in_tile(n):
        m = jnp.minimum(n, n_tiles - 1)
        return (m // tpb, m % tpb, 0)

    def out_tile(n):
        m = jnp.maximum(n - 1, 0)
        return (m // tpb, m % tpb, 0)

    def out_tile_t(n):
        m = jnp.maximum(n - 1, 0)
        return (m // tpb, 0, m % tpb)

    tile = pl.BlockSpec((1, TM, SEC), out_tile)
    tile_p = pl.BlockSpec((1, TM // 2, SEC), out_tile)
    tile_tp = pl.BlockSpec((1, SEC // 2, TM), out_tile_t)
    sds = jax.ShapeDtypeStruct((bsz, seq, SEC), BF16)
    sds_p = jax.ShapeDtypeStruct((bsz, seq // 2, SEC), jnp.uint32)
    sds_tp = jax.ShapeDtypeStruct((bsz, SEC // 2, seq), jnp.uint32)
    return pl.pallas_call(
        functools.partial(_inproj_kernel, tpb),
        grid=(n_tiles + 1,),
        in_specs=[
            pl.BlockSpec((1, TM, D_MODEL), in_tile),
            pl.BlockSpec((1, D_MODEL), const2),
            pl.BlockSpec((D_MODEL // 2, d_in), const2, pipeline_mode=pl.Buffered(1)),
            pl.BlockSpec((CONV_WIDTH, LRU_WIDTH), const2),
            pl.BlockSpec((1, LRU_WIDTH), const2),
            pl.BlockSpec((LRU_BLOCKS, LRU_BLOCK, 2 * LRU_BLOCK), const3),
            pl.BlockSpec((LRU_BLOCKS, 1, 2 * LRU_BLOCK), const3),
            pl.BlockSpec((1, LRU_WIDTH), const2),
            pl.BlockSpec((1, SEC), const2),
            pl.BlockSpec((1, SEC), const2),
            pl.BlockSpec((256, 256), const2),
        ],
        out_specs=[tile, tile_tp, tile_p, tile_tp, tile],
        out_shape=[sds, sds_tp, sds_p, sds_tp, sds],
        scratch_shapes=[
            pltpu.VMEM((TM + SUBLANES, LRU_WIDTH), F32),
            pltpu.VMEM((TM, LRU_WIDTH), F32),
            pltpu.VMEM((TM, LRU_WIDTH), F32),
            pltpu.VMEM((1, LRU_WIDTH), F32),
            pltpu.VMEM((TM, d_in), F32),
        ],
        compiler_params=pltpu.CompilerParams(
            dimension_semantics=("arbitrary",),
            vmem_limit_bytes=VMEM_LIMIT),
        name="inproj_lru_qknorm",
    )(x, ng, w_in, cw, cb, wg, bg, lam, gq, gk, gsum)


def _bucket_table():
    rel = np.arange(TA)[None, :] - np.arange(2 * TA)[:, None] + TA
    n = np.maximum(rel, 0)
    max_exact = N_BUCKETS // 2
    nf = np.maximum(n, 1).astype(np.float32)
    large = max_exact + (np.log(nf / np.float32(max_exact)) / np.float32(math.log(MAX_DISTANCE / max_exact))
                         * np.float32(N_BUCKETS - max_exact)).astype(np.int32)
    large = np.minimum(large, N_BUCKETS - 1)
    bucket = np.where(n < max_exact, n, large)
    return np.where(rel >= 0, bucket, -1).astype(np.int32)


def _attn_kernel(rb_ref, bucket_ref, lq1_ref, lk1_ref, lq2_ref, lk2_ref, sgain_ref,
                 qt_ref, qn_ref, k_ref, vt_ref, sg_ref, o_ref,
                 w_ref, m_ref, acc_ref, sa_ref, sb_ref, ma_ref, mb_ref):
    b = pl.program_id(0)
    i = pl.program_id(1)

    @pl.when((b == 0) & (i == 0))
    def _():
        bk = bucket_ref[...]
        for h in range(ATT_HEADS):
            far = rb_ref[N_BUCKETS - 1, h]
            w = jnp.full(bk.shape, NEG_INF, F32)
            for kk in range(N_BUCKETS):
                w = jnp.where(bk == kk, (rb_ref[kk, h] - far) * LOG2E, w)
            w_ref[h] = w

    lam = (jnp.exp(jnp.sum(lq1_ref[...] * lk1_ref[...], axis=-1, keepdims=True))
           - jnp.exp(jnp.sum(lq2_ref[...] * lk2_ref[...], axis=-1, keepdims=True))
           + LAM_INIT)

    zeros_half = jnp.zeros((HEAD_DIM, TA), BF16)

    def query_operand(q_packed):
        qh = pltpu.bitcast(q_packed, BF16)
        return jnp.concatenate(
            [jnp.concatenate([qh[0:HEAD_DIM, :], zeros_half], axis=0),
             jnp.concatenate([zeros_half, qh[HEAD_DIM:V_DIM, :]], axis=0)], axis=1)

    head_rows = [slice(h * HEAD_DIM, (h + 1) * HEAD_DIM) for h in range(ATT_HEADS)]
    qq = [query_operand(qt_ref[0, head_rows[h], :]) for h in range(ATT_HEADS)]

    ones_rows = (lax.broadcasted_iota(jnp.int32, (SUM_ROWS, TA), 0) == 0).astype(BF16)

    m_ref[...] = jnp.full(m_ref.shape, NEG_INF, F32)
    acc_ref[...] = jnp.zeros(acc_ref.shape, F32)

    last_tile = 2 * k_ref.shape[1] // TA - 1

    def scores_head(h, t, buf, cmax, q_operand=None):
        k0 = pl.multiple_of(jnp.minimum(t, last_tile) * (TA // 2), TA // 2)
        kt = pltpu.bitcast(k_ref[0, pl.ds(k0, TA // 2), h * V_DIM:(h + 1) * V_DIM],
                           BF16)
        qop = qq[h] if q_operand is None else q_operand
        s = jnp.dot(kt, qop, preferred_element_type=F32)
        buf[h] = s
        cmax[h] = jnp.max(s, axis=0, keepdims=True)

    def add_bias(t, buf, cmax):
        @pl.when(t == i)
        def _():
            for h in range(ATT_HEADS):
                w = w_ref[h, TA:2 * TA, :]
                s = buf[h] + jnp.concatenate([w, w], axis=1)
                buf[h] = s
                cmax[h] = jnp.max(s, axis=0, keepdims=True)

        @pl.when(t == i - 1)
        def _():
            hq = TA // 2
            for h in range(ATT_HEADS):
                w = w_ref[h, hq:TA, 0:hq]
                for c0 in (0, TA):
                    lo = buf[h, hq:TA, c0:c0 + hq] + w
                    buf[h, hq:TA, c0:c0 + hq] = lo
                    top = jnp.max(buf[h, 0:hq, c0:c0 + hq], axis=0, keepdims=True)
                    cmax[h, :, c0:c0 + hq] = jnp.maximum(
                        top, jnp.max(lo, axis=0, keepdims=True))

    def softmax_pv_head(h, t, buf, cmax):
        k0 = pl.multiple_of(t * TA, TA)
        s = buf[h]
        m_prev = m_ref[h]
        m_new = jnp.maximum(m_prev, cmax[h])
        alpha = jnp.exp2(m_prev - m_new)
        p = jnp.exp2(s - m_new).astype(BF16)
        vt = pltpu.bitcast(vt_ref[0, h * HEAD_DIM:(h + 1) * HEAD_DIM, pl.ds(k0, TA)],
                           BF16)
        vt = jnp.concatenate([vt, ones_rows], axis=0)
        acc_ref[h] = alpha * acc_ref[h] + jnp.dot(vt, p, preferred_element_type=F32)
        m_ref[h] = m_new

    def stage(t_cur, cur, t_next, nxt):
        for h in range(ATT_HEADS):
            scores_head(h, t_next, *nxt)
            softmax_pv_head(h, t_cur, *cur)
        add_bias(t_next, *nxt)

    buf_a = (sa_ref, ma_ref)
    buf_b = (sb_ref, mb_ref)
    n_tiles = i + 1

    @pl.when(i == 0)
    def _():
        for h in range(ATT_HEADS):
            scores_head(h, 0, *buf_a)

    add_bias(0, *buf_a)

    def pair_body(u, c):
        t0 = 2 * u
        stage(t0, buf_a, t0 + 1, buf_b)
        stage(t0 + 1, buf_b, t0 + 2, buf_a)
        return c

    n_pairs = n_tiles // 2
    lax.fori_loop(0, n_pairs - 1, pair_body, 0)

    @pl.when(n_pairs >= 1)
    def _():
        t0 = 2 * (n_pairs - 1)
        stage(t0, buf_a, t0 + 1, buf_b)
        is_next = t0 + 2 >= n_tiles
        t_next = jnp.where(is_next, 0, t0 + 2)
        for h in range(ATT_HEADS):
            q_packed = jnp.where(is_next, qn_ref[0, head_rows[h], :], qt_ref[0, head_rows[h], :])
            scores_head(h, t_next, *buf_a, q_operand=query_operand(q_packed))
            softmax_pv_head(h, t0 + 1, *buf_b)
        add_bias(t0 + 2, *buf_a)

    @pl.when(n_tiles % 2 == 1)
    def _():
        for h in range(ATT_HEADS):
            softmax_pv_head(h, n_tiles - 1, *buf_a)
            scores_head(h, 0, *buf_a, q_operand=query_operand(qn_ref[0, head_rows[h], :]))

    gain = sgain_ref[...] * (1.0 - LAM_INIT)
    for h in range(ATT_HEADS):
        hs = slice(h * V_DIM, (h + 1) * V_DIM)
        o = acc_ref[h, 0:V_DIM, :] * (1.0 / acc_ref[h, V_DIM:V_DIM + 1, :])
        o = o[:, 0:TA] - lam * o[:, TA:2 * TA]
        ms = jnp.mean(o * o, axis=0, keepdims=True)
        y = (o * lax.rsqrt(ms + EPS)).T * gain
        o_ref[0, :, hs] = (y * sg_ref[0, :, hs].astype(F32)).astype(BF16)


def _attention(rel_bias, lq1, lk1, lq2, lk2, sgain, qt, k, vt, sg):
    bsz, seq, width = sg.shape
    table = _bucket_table()
    near = table[:TA] != N_BUCKETS - 1
    assert not near[:TA // 2].any() and not near[:, TA // 2:].any()
    bucket = jnp.asarray(table)
    const2 = lambda b, i: (0, 0)
    return pl.pallas_call(
        _attn_kernel,
        grid=(bsz, seq // TA),
        in_specs=[
            pl.BlockSpec(memory_space=pltpu.SMEM),
            pl.BlockSpec((2 * TA, TA), const2),
            pl.BlockSpec((1, HEAD_DIM), const2),
            pl.BlockSpec((1, HEAD_DIM), const2),
            pl.BlockSpec((1, HEAD_DIM), const2),
            pl.BlockSpec((1, HEAD_DIM), const2),
            pl.BlockSpec((1, V_DIM), const2),
            pl.BlockSpec((1, width // 2, TA), lambda b, i: (b, 0, i)),
            pl.BlockSpec((1, width // 2, TA),
                         lambda b, i: (b, 0, jnp.minimum(i + 1, seq // TA - 1))),
            pl.BlockSpec((1, seq // 2, width), lambda b, i: (b, 0, 0)),
            pl.BlockSpec((1, width // 2, seq), lambda b, i: (b, 0, 0)),
            pl.BlockSpec((1, TA, width), lambda b, i: (b, i, 0)),
        ],
        out_specs=pl.BlockSpec((1, TA, width), lambda b, i: (b, i, 0)),
        out_shape=jax.ShapeDtypeStruct((bsz, seq, width), BF16),
        scratch_shapes=[
            pltpu.VMEM((ATT_HEADS, 2 * TA, TA), F32),
            pltpu.VMEM((ATT_HEADS, 1, 2 * TA), F32),
            pltpu.VMEM((ATT_HEADS, V_DIM + SUM_ROWS, 2 * TA), F32),
            pltpu.VMEM((ATT_HEADS, TA, 2 * TA), F32),
            pltpu.VMEM((ATT_HEADS, TA, 2 * TA), F32),
            pltpu.VMEM((ATT_HEADS, 1, 2 * TA), F32),
            pltpu.VMEM((ATT_HEADS, 1, 2 * TA), F32),
        ],
        compiler_params=pltpu.CompilerParams(
            dimension_semantics=("arbitrary", "arbitrary"),
            vmem_limit_bytes=VMEM_LIMIT),
        name="diff_attention",
    )(rel_bias, bucket, lq1, lk1, lq2, lk2, sgain, qt, qt, k, vt, sg)


def _outproj_kernel(x_ref, ya_ref, yb_ref, w_ref, o_ref):
    acc = jnp.dot(ya_ref[...], w_ref[0:LRU_WIDTH, :], preferred_element_type=F32)
    acc = acc + jnp.dot(yb_ref[...], w_ref[LRU_WIDTH:, :], preferred_element_type=F32)
    o_ref[...] = x_ref[...] + acc


def _outproj(x2, ya, yb, w_out):
    n, d = x2.shape
    row = lambda i: (i, 0)
    return pl.pallas_call(
        _outproj_kernel,
        grid=(n // TM_OUT,),
        in_specs=[
            pl.BlockSpec((TM_OUT, d), row),
            pl.BlockSpec((TM_OUT, ya.shape[1]), row),
            pl.BlockSpec((TM_OUT, yb.shape[1]), row),
            pl.BlockSpec(w_out.shape, lambda i: (0, 0)),
        ],
        out_specs=pl.BlockSpec((TM_OUT, d), row),
        out_shape=jax.ShapeDtypeStruct((n, d), F32),
        compiler_params=pltpu.CompilerParams(
            dimension_semantics=("arbitrary",), vmem_limit_bytes=VMEM_LIMIT),
        name="outproj_residual",
    )(x2, ya, yb, w_out)


def _pack_rows(w):
    rows, cols = w.shape
    slabs_per_section = SEC // SLAB

    def pack_kernel(w_ref, o_ref):
        o_ref[...] = _packed(w_ref[...])

    return pl.pallas_call(
        pack_kernel,
        grid=(cols // SLAB,),
        in_specs=[pl.BlockSpec(
            (rows, SLAB), lambda j: (0, (j % N_SECTIONS) * slabs_per_section + j // N_SECTIONS))],
        out_specs=pl.BlockSpec((rows // 2, SLAB), lambda j: (0, j)),
        out_shape=jax.ShapeDtypeStruct((rows // 2, cols), jnp.uint32),
        name="pack_weight_rows",
    )(w)


def kernel(x, norm_gain, w_in, conv_w, conv_b, w_rg, b_rg, w_ig, b_ig, lru_lambda,
           q_norm_gain, k_norm_gain, lambda_q1, lambda_k1, lambda_q2, lambda_k2,
           subln_gain, w_out, rel_bias):
    bsz, seq, d = x.shape
    assert norm_gain.shape[0] == 1, "single layer only"
    assert seq % TM == 0 and seq % TA == 0 and (bsz * seq) % TM_OUT == 0
    n_groups = SEC // HEAD_DIM
    wg = (0.5 * jnp.concatenate([w_rg[0], w_ig[0]], axis=-1)).astype(BF16)
    bg = 0.5 * jnp.concatenate([b_rg[0], b_ig[0]], axis=-1)[:, None, :]
    gq = jnp.tile(q_norm_gain[0], n_groups)[None, :]
    gk = jnp.tile(k_norm_gain[0], n_groups)[None, :]
    gsum = jnp.asarray(np.kron(np.eye(256 // HEAD_DIM), np.ones((HEAD_DIM, HEAD_DIM))) / HEAD_DIM,
                       BF16)

    ylru, qt, k, vt, sg = _inproj(
        x, norm_gain, _pack_rows(w_in[0]), conv_w[0], conv_b, wg, bg, lru_lambda,
        gq, gk, gsum)
    yatt = _attention(rel_bias, lambda_q1, lambda_k1, lambda_q2, lambda_k2, subln_gain,
                      qt, k, vt, sg)
    out = _outproj(x.reshape(bsz * seq, d), ylru.reshape(bsz * seq, -1),
                   yatt.reshape(bsz * seq, -1), w_out[0].astype(BF16))
    return out.reshape(bsz, seq, d)
```

```python
import functools
import math

import jax
import jax.numpy as jnp
import numpy as np
from jax import lax
from jax.experimental import pallas as pl
from jax.experimental.pallas import tpu as pltpu

F32 = jnp.float32
BF16 = jnp.bfloat16

D_MODEL = 1024
LRU_WIDTH = 1024
LRU_BLOCKS = 8
LRU_BLOCK = LRU_WIDTH // LRU_BLOCKS
CONV_WIDTH = 4
LRU_C = 8.0
ATT_HEADS = 8
HEAD_DIM = 64
V_DIM = 2 * HEAD_DIM
SEC = 1024
N_BUCKETS = 32
MAX_DISTANCE = 128
EPS = 1e-6
NEG_INF = -1e30
TINY = 1e-30
LAM_INIT = 0.8 - 0.6 * math.exp(-0.3 * 0)
LOG2E = math.log2(math.e)

SUBLANES = 8
TM = 512
TM_OUT = 1024
SLAB = 256
PACK_COLS = 512
TA = 256
SUM_ROWS = 16
VMEM_LIMIT = 58 * 1024 * 1024


def _silu(x):
    hx = 0.5 * x
    return hx + hx * jnp.tanh(hx)


def _packed(x):
    return pltpu.bitcast(x.astype(BF16), jnp.uint32)


def _inproj_kernel(tiles_per_batch,
                   x_ref, ng_ref, w_ref, cw_ref, cb_ref, wg_ref, bg_ref, lam_ref,
                   gq_ref, gk_ref, gsum_ref,
                   ylru_ref, qt_ref, k_ref, vt_ref, sg_ref,
                   xbuf, a_buf, u_buf, hcar, zbuf):
    step = pl.program_id(0)

    @pl.when(step == 0)
    def _():
        zbuf[...] = jnp.zeros_like(zbuf)

    @pl.when(jnp.maximum(step - 1, 0) % tiles_per_batch == 0)
    def _():
        xbuf[0:SUBLANES, :] = jnp.zeros((SUBLANES, LRU_WIDTH), F32)
        hcar[...] = jnp.zeros_like(hcar)

    x = x_ref[0]
    ms = jnp.mean(x * x, axis=-1, keepdims=True)
    hb = (x * lax.rsqrt(ms + EPS) * ng_ref[...]).astype(BF16)

    def proj(c, s):
        col = c * SEC + s * SLAB
        prev = zbuf[:, col:col + SLAB]
        w_slab = pltpu.bitcast(w_ref[:, col:col + SLAB], BF16)
        zbuf[:, col:col + SLAB] = jnp.dot(hb, w_slab, preferred_element_type=F32)
        return prev

    def qk_norm(zc, gain):
        sq = (zc * zc).astype(BF16)
        mean_sq = jnp.dot(sq, gsum_ref[...], preferred_element_type=F32)
        return zc * lax.rsqrt(mean_sq + EPS) * gain

    z = -lam_ref[...]
    softplus = jnp.maximum(z, 0.0) + jnp.log1p(jnp.exp(-jnp.abs(z)))
    half_log2_a_max = (-0.5 * LRU_C * LOG2E) * softplus
    gq = gq_ref[...] * (LOG2E / math.sqrt(HEAD_DIM))
    row = lax.broadcasted_iota(jnp.int32, (SUBLANES, SLAB), 0)
    n_slabs = LRU_WIDTH // SLAB

    xl_next = proj(0, 0)
    for s in range(n_slabs):
        ss = slice(s * SLAB, (s + 1) * SLAB)
        xl = xl_next
        if s + 1 < n_slabs:
            xl_next = proj(0, s + 1)

        xbuf[SUBLANES:SUBLANES + TM, ss] = xl
        xc = cw_ref[3:4, ss] * xl + cb_ref[:, ss]
        for d in range(1, CONV_WIDTH):
            xc = xc + cw_ref[3 - d:4 - d, ss] * xbuf[SUBLANES - d:SUBLANES - d + TM, ss]
        xbuf[0:SUBLANES, ss] = xbuf[TM:TM + SUBLANES, ss]

        xcb = xc.astype(BF16)
        for j in range(SLAB // LRU_BLOCK):
            blk = s * (SLAB // LRU_BLOCK) + j
            cs = slice(blk * LRU_BLOCK, (blk + 1) * LRU_BLOCK)
            js = slice(j * LRU_BLOCK, (j + 1) * LRU_BLOCK)
            th = jnp.tanh(jnp.dot(xcb[:, js], wg_ref[blk], preferred_element_type=F32)
                          + bg_ref[blk])
            c2 = half_log2_a_max[:, cs]
            a = jnp.exp2(c2 * th[:, :LRU_BLOCK] + c2)
            hxc = 0.5 * xc[:, js]
            gated_x = hxc * th[:, LRU_BLOCK:] + hxc
            y = 1.0 - a * a
            a_buf[:, cs] = a
            u_buf[:, cs] = (y * lax.rsqrt(jnp.maximum(y, TINY))) * gated_x
            if j == 0:
                zq = proj(2, s)

        carry = hcar[:, ss]
        n_groups = TM // SUBLANES
        for g in range(n_groups):
            if g == 0:
                zk = proj(3, s)
            elif g == n_groups // 3:
                gl = proj(1, s)
            elif g == 2 * n_groups // 3:
                zv = proj(4, s)
            rs = slice(g * SUBLANES, (g + 1) * SUBLANES)
            a = a_buf[rs, ss]
            u = u_buf[rs, ss]
            for k in (1, 2, 4):
                keep = row >= k
                a_sh = jnp.where(keep, pltpu.roll(a, k, 0), 1.0)
                u_sh = jnp.where(keep, pltpu.roll(u, k, 0), 0.0)
                u = a * u_sh + u
                a = a * a_sh
            h = a * carry + u
            u_buf[rs, ss] = h
            carry = h[SUBLANES - 1:SUBLANES, :]
        hcar[:, ss] = carry

        zg = proj(5, s)
        qt_ref[0, s * SLAB // 2:(s + 1) * SLAB // 2, :] = _packed(qk_norm(zq, gq[:, ss]).T)
        k_ref[0, :, ss] = _packed(qk_norm(zk, gk_ref[:, ss]))
        ylru_ref[0, :, ss] = (u_buf[:, ss] * _silu(gl)).astype(BF16)
        a_buf[:, ss] = zv
        vt_ref[0, s * SLAB // 2:(s + 1) * SLAB // 2, :] = _packed(a_buf[:, ss].T)
        sg_ref[0, :, ss] = _silu(zg).astype(BF16)


def _inproj(x, ng, w_in, cw, cb, wg, bg, lam, gq, gk, gsum):
    bsz, seq, _ = x.shape
    d_in = w_in.shape[1]
    tpb = seq // TM
    n_tiles = bsz * tpb
    const2 = lambda n: (0, 0)
    const3 = lambda n: (0, 0, 0)

    def in_tile(n):
        m = jnp.minimum(n, n_tiles - 1)
        return (m // tpb, m % tpb, 0)

    def out_tile(n):
        m = jnp.maximum(n - 1, 0)
        return (m // tpb, m % tpb, 0)

    def out_tile_t(n):
        m = jnp.maximum(n - 1, 0)
        return (m // tpb, 0, m % tpb)

    tile = pl.BlockSpec((1, TM, SEC), out_tile)
    tile_p = pl.BlockSpec((1, TM // 2, SEC), out_tile)
    tile_tp = pl.BlockSpec((1, SEC // 2, TM), out_tile_t)
    sds = jax.ShapeDtypeStruct((bsz, seq, SEC), BF16)
    sds_p = jax.ShapeDtypeStruct((bsz, seq // 2, SEC), jnp.uint32)
    sds_tp = jax.ShapeDtypeStruct((bsz, SEC // 2, seq), jnp.uint32)
    return pl.pallas_call(
        functools.partial(_inproj_kernel, tpb),
        grid=(n_tiles + 1,),
        in_specs=[
            pl.BlockSpec((1, TM, D_MODEL), in_tile),
            pl.BlockSpec((1, D_MODEL), const2),
            pl.BlockSpec((D_MODEL // 2, d_in), const2, pipeline_mode=pl.Buffered(1)),
            pl.BlockSpec((CONV_WIDTH, LRU_WIDTH), const2),
            pl.BlockSpec((1, LRU_WIDTH), const2),
            pl.BlockSpec((LRU_BLOCKS, LRU_BLOCK, 2 * LRU_BLOCK), const3),
            pl.BlockSpec((LRU_BLOCKS, 1, 2 * LRU_BLOCK), const3),
            pl.BlockSpec((1, LRU_WIDTH), const2),
            pl.BlockSpec((1, SEC), const2),
            pl.BlockSpec((1, SEC), const2),
            pl.BlockSpec((256, 256), const2),
        ],
        out_specs=[tile, tile_tp, tile_p, tile_tp, tile],
        out_shape=[sds, sds_tp, sds_p, sds_tp, sds],
        scratch_shapes=[
            pltpu.VMEM((TM + SUBLANES, LRU_WIDTH), F32),
            pltpu.VMEM((TM, LRU_WIDTH), F32),
            pltpu.VMEM((TM, LRU_WIDTH), F32),
            pltpu.VMEM((1, LRU_WIDTH), F32),
            pltpu.VMEM((TM, d_in), F32),
        ],
        compiler_params=pltpu.CompilerParams(
            dimension_semantics=("arbitrary",),
            vmem_limit_bytes=VMEM_LIMIT),
        name="inproj_lru_qknorm",
    )(x, ng, w_in, cw, cb, wg, bg, lam, gq, gk, gsum)


def _bucket_table():
    rel = np.arange(TA)[None, :] - np.arange(2 * TA)[:, None] + TA
    n = np.maximum(rel, 0)
    max_exact = N_BUCKETS // 2
    nf = np.maximum(n, 1).astype(np.float32)
    large = max_exact + (np.log(nf / np.float32(max_exact)) / np.float32(math.log(MAX_DISTANCE / max_exact))
                         * np.float32(N_BUCKETS - max_exact)).astype(np.int32)
    large = np.minimum(large, N_BUCKETS - 1)
    bucket = np.where(n < max_exact, n, large)
    return np.where(rel >= 0, bucket, -1).astype(np.int32)


def _attn_kernel(rb_ref, bucket_ref, lq1_ref, lk1_ref, lq2_ref, lk2_ref, sgain_ref,
                 qt_ref, qn_ref, k_ref, vt_ref, sg_ref, o_ref,
                 w_ref, m_ref, acc_ref, sa_ref, sb_ref, ma_ref, mb_ref):
    b = pl.program_id(0)
    i = pl.program_id(1)

    @pl.when((b == 0) & (i == 0))
    def _():
        bk = bucket_ref[...]
        for h in range(ATT_HEADS):
            far = rb_ref[N_BUCKETS - 1, h]
            w = jnp.full(bk.shape, NEG_INF, F32)
            for kk in range(N_BUCKETS):
                w = jnp.where(bk == kk, (rb_ref[kk, h] - far) * LOG2E, w)
            w_ref[h] = w
        acc_ref[...] = jnp.zeros(acc_ref.shape, F32)

    lam = (jnp.exp(jnp.sum(lq1_ref[...] * lk1_ref[...], axis=-1, keepdims=True))
           - jnp.exp(jnp.sum(lq2_ref[...] * lk2_ref[...], axis=-1, keepdims=True))
           + LAM_INIT)

    zeros_half = jnp.zeros((HEAD_DIM, TA), BF16)

    def query_operand(q_packed):
        qh = pltpu.bitcast(q_packed, BF16)
        return jnp.concatenate(
            [jnp.concatenate([qh[0:HEAD_DIM, :], zeros_half], axis=0),
             jnp.concatenate([zeros_half, qh[HEAD_DIM:V_DIM, :]], axis=0)], axis=1)

    head_rows = [slice(h * HEAD_DIM, (h + 1) * HEAD_DIM) for h in range(ATT_HEADS)]
    qq = [query_operand(qt_ref[0, head_rows[h], :]) for h in range(ATT_HEADS)]

    ones_rows = (lax.broadcasted_iota(jnp.int32, (SUM_ROWS, TA), 0) == 0).astype(BF16)

    m_ref[...] = jnp.full(m_ref.shape, NEG_INF, F32)

    last_tile = 2 * k_ref.shape[1] // TA - 1

    def scores_head(h, t, buf, cmax, q_operand=None):
        k0 = pl.multiple_of(jnp.minimum(t, last_tile) * (TA // 2), TA // 2)
        kt = pltpu.bitcast(k_ref[0, pl.ds(k0, TA // 2), h * V_DIM:(h + 1) * V_DIM],
                           BF16)
        qop = qq[h] if q_operand is None else q_operand
        s = jnp.dot(kt, qop, preferred_element_type=F32)
        buf[h] = s
        cmax[h] = jnp.max(s, axis=0, keepdims=True)

    def add_bias(t, buf, cmax):
        @pl.when(t == i)
        def _():
            for h in range(ATT_HEADS):
                w = w_ref[h, TA:2 * TA, :]
                s = buf[h] + jnp.concatenate([w, w], axis=1)
                buf[h] = s
                cmax[h] = jnp.max(s, axis=0, keepdims=True)

        @pl.when(t == i - 1)
        def _():
            hq = TA // 2
            for h in range(ATT_HEADS):
                w = w_ref[h, hq:TA, 0:hq]
                for c0 in (0, TA):
                    lo = buf[h, hq:TA, c0:c0 + hq] + w
                    buf[h, hq:TA, c0:c0 + hq] = lo
                    top = jnp.max(buf[h, 0:hq, c0:c0 + hq], axis=0, keepdims=True)
                    cmax[h, :, c0:c0 + hq] = jnp.maximum(
                        top, jnp.max(lo, axis=0, keepdims=True))

    def softmax_pv_head(h, t, buf, cmax):
        k0 = pl.multiple_of(t * TA, TA)
        s = buf[h]
        m_prev = m_ref[h]
        m_new = jnp.maximum(m_prev, cmax[h])
        alpha = jnp.exp2(m_prev - m_new)
        p = jnp.exp2(s - m_new).astype(BF16)
        vt = pltpu.bitcast(vt_ref[0, h * HEAD_DIM:(h + 1) * HEAD_DIM, pl.ds(k0, TA)],
                           BF16)
        vt = jnp.concatenate([vt, ones_rows], axis=0)
        acc_ref[h] = alpha * acc_ref[h] + jnp.dot(vt, p, preferred_element_type=F32)
        m_ref[h] = m_new

    def stage(t_cur, cur, t_next, nxt):
        for h in range(ATT_HEADS):
            scores_head(h, t_next, *nxt)
            softmax_pv_head(h, t_cur, *cur)
        add_bias(t_next, *nxt)

    buf_a = (sa_ref, ma_ref)
    buf_b = (sb_ref, mb_ref)
    n_tiles = i + 1

    @pl.when(i == 0)
    def _():
        for h in range(ATT_HEADS):
            scores_head(h, 0, *buf_a)

    add_bias(0, *buf_a)

    def pair_body(u, c):
        t0 = 2 * u
        stage(t0, buf_a, t0 + 1, buf_b)
        stage(t0 + 1, buf_b, t0 + 2, buf_a)
        return c

    n_pairs = n_tiles // 2
    lax.fori_loop(0, n_pairs - 1, pair_body, 0)

    @pl.when(n_pairs >= 1)
    def _():
        t0 = 2 * (n_pairs - 1)
        stage(t0, buf_a, t0 + 1, buf_b)
        is_next = t0 + 2 >= n_tiles
        t_next = jnp.where(is_next, 0, t0 + 2)
        for h in range(ATT_HEADS):
            q_packed = jnp.where(is_next, qn_ref[0, head_rows[h], :], qt_ref[0, head_rows[h], :])
            scores_head(h, t_next, *buf_a, q_operand=query_operand(q_packed))
            softmax_pv_head(h, t0 + 1, *buf_b)
        add_bias(t0 + 2, *buf_a)

    @pl.when(n_tiles % 2 == 1)
    def _():
        for h in range(ATT_HEADS):
            softmax_pv_head(h, n_tiles - 1, *buf_a)
            scores_head(h, 0, *buf_a, q_operand=query_operand(qn_ref[0, head_rows[h], :]))

    gain = sgain_ref[...] * (1.0 - LAM_INIT)
    for h in range(ATT_HEADS):
        hs = slice(h * V_DIM, (h + 1) * V_DIM)
        o = acc_ref[h, 0:V_DIM, :] * (1.0 / acc_ref[h, V_DIM:V_DIM + 1, :])
        o = o[:, 0:TA] - lam * o[:, TA:2 * TA]
        ms = jnp.mean(o * o, axis=0, keepdims=True)
        y = (o * lax.rsqrt(ms + EPS)).T * gain
        o_ref[0, :, hs] = (y * sg_ref[0, :, hs].astype(F32)).astype(BF16)


def _attention(rel_bias, lq1, lk1, lq2, lk2, sgain, qt, k, vt, sg):
    bsz, seq, width = sg.shape
    table = _bucket_table()
    near = table[:TA] != N_BUCKETS - 1
    assert not near[:TA // 2].any() and not near[:, TA // 2:].any()
    bucket = jnp.asarray(table)
    const2 = lambda b, i: (0, 0)
    return pl.pallas_call(
        _attn_kernel,
        grid=(bsz, seq // TA),
        in_specs=[
            pl.BlockSpec(memory_space=pltpu.SMEM),
            pl.BlockSpec((2 * TA, TA), const2),
            pl.BlockSpec((1, HEAD_DIM), const2),
            pl.BlockSpec((1, HEAD_DIM), const2),
            pl.BlockSpec((1, HEAD_DIM), const2),
            pl.BlockSpec((1, HEAD_DIM), const2),
            pl.BlockSpec((1, V_DIM), const2),
            pl.BlockSpec((1, width // 2, TA), lambda b, i: (b, 0, i)),
            pl.BlockSpec((1, width // 2, TA),
                         lambda b, i: (b, 0, jnp.minimum(i + 1, seq // TA - 1))),
            pl.BlockSpec((1, seq // 2, width), lambda b, i: (b, 0, 0)),
            pl.BlockSpec((1, width // 2, seq), lambda b, i: (b, 0, 0)),
            pl.BlockSpec((1, TA, width), lambda b, i: (b, i, 0)),
        ],
        out_specs=pl.BlockSpec((1, TA, width), lambda b, i: (b, i, 0)),
        out_shape=jax.ShapeDtypeStruct((bsz, seq, width), BF16),
        scratch_shapes=[
            pltpu.VMEM((ATT_HEADS, 2 * TA, TA), F32),
            pltpu.VMEM((ATT_HEADS, 1, 2 * TA), F32),
            pltpu.VMEM((ATT_HEADS, V_DIM + SUM_ROWS, 2 * TA), F32),
            pltpu.VMEM((ATT_HEADS, TA, 2 * TA), F32),
            pltpu.VMEM((ATT_HEADS, TA, 2 * TA), F32),
            pltpu.VMEM((ATT_HEADS, 1, 2 * TA), F32),
            pltpu.VMEM((ATT_HEADS, 1, 2 * TA), F32),
        ],
        compiler_params=pltpu.CompilerParams(
            dimension_semantics=("arbitrary", "arbitrary"),
            vmem_limit_bytes=VMEM_LIMIT),
        name="diff_attention",
    )(rel_bias, bucket, lq1, lk1, lq2, lk2, sgain, qt, qt, k, vt, sg)


def _outproj_kernel(x_ref, ya_ref, yb_ref, w_ref, o_ref):
    acc = jnp.dot(ya_ref[...], w_ref[0:LRU_WIDTH, :], preferred_element_type=F32)
    acc = acc + jnp.dot(yb_ref[...], w_ref[LRU_WIDTH:, :], preferred_element_type=F32)
    o_ref[...] = x_ref[...] + acc


def _outproj(x2, ya, yb, w_out):
    n, d = x2.shape
    row = lambda i: (i, 0)
    return pl.pallas_call(
        _outproj_kernel,
        grid=(n // TM_OUT,),
        in_specs=[
            pl.BlockSpec((TM_OUT, d), row),
            pl.BlockSpec((TM_OUT, ya.shape[1]), row),
            pl.BlockSpec((TM_OUT, yb.shape[1]), row),
            pl.BlockSpec(w_out.shape, lambda i: (0, 0)),
        ],
        out_specs=pl.BlockSpec((TM_OUT, d), row),
        out_shape=jax.ShapeDtypeStruct((n, d), F32),
        compiler_params=pltpu.CompilerParams(
            dimension_semantics=("arbitrary",), vmem_limit_bytes=VMEM_LIMIT),
        name="outproj_residual",
    )(x2, ya, yb, w_out)


def _pack_rows(w):
    rows, cols = w.shape

    def pack_kernel(w_ref, o_ref):
        o_ref[...] = _packed(w_ref[...])

    return pl.pallas_call(
        pack_kernel,
        grid=(cols // PACK_COLS,),
        in_specs=[pl.BlockSpec((rows, PACK_COLS), lambda j: (0, j))],
        out_specs=pl.BlockSpec((rows // 2, PACK_COLS), lambda j: (0, j)),
        out_shape=jax.ShapeDtypeStruct((rows // 2, cols), jnp.uint32),
        name="pack_weight_rows",
    )(w)


def kernel(x, norm_gain, w_in, conv_w, conv_b, w_rg, b_rg, w_ig, b_ig, lru_lambda,
           q_norm_gain, k_norm_gain, lambda_q1, lambda_k1, lambda_q2, lambda_k2,
           subln_gain, w_out, rel_bias):
    bsz, seq, d = x.shape
    assert norm_gain.shape[0] == 1, "single layer only"
    assert seq % TM == 0 and seq % TA == 0 and (bsz * seq) % TM_OUT == 0
    n_groups = SEC // HEAD_DIM
    wg = (0.5 * jnp.concatenate([w_rg[0], w_ig[0]], axis=-1)).astype(BF16)
    bg = 0.5 * jnp.concatenate([b_rg[0], b_ig[0]], axis=-1)[:, None, :]
    gq = jnp.tile(q_norm_gain[0], n_groups)[None, :]
    gk = jnp.tile(k_norm_gain[0], n_groups)[None, :]
    gsum = jnp.asarray(np.kron(np.eye(256 // HEAD_DIM), np.ones((HEAD_DIM, HEAD_DIM))) / HEAD_DIM,
                       BF16)

    ylru, qt, k, vt, sg = _inproj(
        x, norm_gain, _pack_rows(w_in[0]), conv_w[0], conv_b, wg, bg, lru_lambda,
        gq, gk, gsum)
    yatt = _attention(rel_bias, lambda_q1, lambda_k1, lambda_q2, lambda_k2, subln_gain,
                      qt, k, vt, sg)
    out = _outproj(x.reshape(bsz * seq, d), ylru.reshape(bsz * seq, -1),
                   yatt.reshape(bsz * seq, -1), w_out[0].astype(BF16))
    return out.reshape(bsz, seq, d)
```

```python
import functools
import math

import jax
import jax.numpy as jnp
import numpy as np
from jax import lax
from jax.experimental import pallas as pl
from jax.experimental.pallas import tpu as pltpu

F32 = jnp.float32
BF16 = jnp.bfloat16

D_MODEL = 1024
LRU_WIDTH = 1024
LRU_BLOCKS = 8
LRU_BLOCK = LRU_WIDTH // LRU_BLOCKS
CONV_WIDTH = 4
LRU_C = 8.0
ATT_HEADS = 8
HEAD_DIM = 64
V_DIM = 2 * HEAD_DIM
SEC = 1024
N_BUCKETS = 32
MAX_DISTANCE = 128
EPS = 1e-6
NEG_INF = -1e30
TINY = 1e-30
LAM_INIT = 0.8 - 0.6 * math.exp(-0.3 * 0)
LOG2E = math.log2(math.e)

SUBLANES = 8
TM = 512
TM_OUT = 1024
SLAB = 256
PACK_COLS = 512
TA = 256
SUM_ROWS = 16
VMEM_LIMIT = 58 * 1024 * 1024


def _silu(x):
    hx = 0.5 * x
    return hx + hx * jnp.tanh(hx)


def _packed(x):
    return pltpu.bitcast(x.astype(BF16), jnp.uint32)


def _inproj_kernel(tiles_per_batch,
                   x_ref, ng_ref, w_ref, cw_ref, cb_ref, wg_ref, bg_ref, lam_ref,
                   gq_ref, gk_ref, gsum_ref,
                   ylru_ref, qt_ref, k_ref, vt_ref, sg_ref,
                   xbuf, a_buf, u_buf, hcar, zbuf):
    step = pl.program_id(0)

    @pl.when(step == 0)
    def _():
        zbuf[...] = jnp.zeros_like(zbuf)

    @pl.when(jnp.maximum(step - 1, 0) % tiles_per_batch == 0)
    def _():
        xbuf[0:SUBLANES, :] = jnp.zeros((SUBLANES, LRU_WIDTH), F32)
        hcar[...] = jnp.zeros_like(hcar)

    x = x_ref[0]
    ms = jnp.mean(x * x, axis=-1, keepdims=True)
    hb = (x * lax.rsqrt(ms + EPS) * ng_ref[...]).astype(BF16)

    def proj(c, s):
        col = c * SEC + s * SLAB
        prev = zbuf[:, col:col + SLAB]
        w_slab = pltpu.bitcast(w_ref[:, col:col + SLAB], BF16)
        zbuf[:, col:col + SLAB] = jnp.dot(hb, w_slab, preferred_element_type=F32)
        return prev

    def qk_norm(zc, gain):
        sq = (zc * zc).astype(BF16)
        mean_sq = jnp.dot(sq, gsum_ref[...], preferred_element_type=F32)
        return zc * lax.rsqrt(mean_sq + EPS) * gain

    z = -lam_ref[...]
    softplus = jnp.maximum(z, 0.0) + jnp.log1p(jnp.exp(-jnp.abs(z)))
    half_log2_a_max = (-0.5 * LRU_C * LOG2E) * softplus
    gq = gq_ref[...] * (LOG2E / math.sqrt(HEAD_DIM))
    row = lax.broadcasted_iota(jnp.int32, (SUBLANES, SLAB), 0)
    n_slabs = LRU_WIDTH // SLAB

    xl_next = proj(0, 0)
    for s in range(n_slabs):
        ss = slice(s * SLAB, (s + 1) * SLAB)
        xl = xl_next
        if s + 1 < n_slabs:
            xl_next = proj(0, s + 1)

        xbuf[SUBLANES:SUBLANES + TM, ss] = xl
        xc = cw_ref[3:4, ss] * xl + cb_ref[:, ss]
        for d in range(1, CONV_WIDTH):
            xc = xc + cw_ref[3 - d:4 - d, ss] * xbuf[SUBLANES - d:SUBLANES - d + TM, ss]
        xbuf[0:SUBLANES, ss] = xbuf[TM:TM + SUBLANES, ss]

        xcb = xc.astype(BF16)
        for j in range(SLAB // LRU_BLOCK):
            blk = s * (SLAB // LRU_BLOCK) + j
            cs = slice(blk * LRU_BLOCK, (blk + 1) * LRU_BLOCK)
            js = slice(j * LRU_BLOCK, (j + 1) * LRU_BLOCK)
            th = jnp.tanh(jnp.dot(xcb[:, js], wg_ref[blk], preferred_element_type=F32)
                          + bg_ref[blk])
            c2 = half_log2_a_max[:, cs]
            a = jnp.exp2(c2 * th[:, :LRU_BLOCK] + c2)
            hxc = 0.5 * xc[:, js]
            gated_x = hxc * th[:, LRU_BLOCK:] + hxc
            y = 1.0 - a * a
            a_buf[:, cs] = a
            u_buf[:, cs] = (y * lax.rsqrt(jnp.maximum(y, TINY))) * gated_x
            if j == 0:
                zq = proj(2, s)

        carry = hcar[:, ss]
        n_groups = TM // SUBLANES
        for g in range(n_groups):
            if g == 0:
                zk = proj(3, s)
            elif g == n_groups // 3:
                gl = proj(1, s)
            elif g == 2 * n_groups // 3:
                zv = proj(4, s)
            rs = slice(g * SUBLANES, (g + 1) * SUBLANES)
            a = a_buf[rs, ss]
            u = u_buf[rs, ss]
            for k in (1, 2, 4):
                keep = row >= k
                a_sh = jnp.where(keep, pltpu.roll(a, k, 0), 1.0)
                u_sh = jnp.where(keep, pltpu.roll(u, k, 0), 0.0)
                u = a * u_sh + u
                a = a * a_sh
            h = a * carry + u
            u_buf[rs, ss] = h
            carry = h[SUBLANES - 1:SUBLANES, :]
        hcar[:, ss] = carry

        zg = proj(5, s)
        qt_ref[0, s * SLAB // 2:(s + 1) * SLAB // 2, :] = _packed(qk_norm(zq, gq[:, ss]).T)
        k_ref[0, :, ss] = _packed(qk_norm(zk, gk_ref[:, ss]))
        ylru_ref[0, :, ss] = (u_buf[:, ss] * _silu(gl)).astype(BF16)
        a_buf[:, ss] = zv
        vt_ref[0, s * SLAB // 2:(s + 1) * SLAB // 2, :] = _packed(a_buf[:, ss].T)
        sg_ref[0, :, ss] = _silu(zg).astype(BF16)


def _inproj(x, ng, w_in, cw, cb, wg, bg, lam, gq, gk, gsum):
    bsz, seq, _ = x.shape
    d_in = w_in.shape[1]
    tpb = seq // TM
    n_tiles = bsz * tpb
    const2 = lambda n: (0, 0)
    const3 = lambda n: (0, 0, 0)

    def in_tile(n):
        m = jnp.minimum(n, n_tiles - 1)
        return (m // tpb, m % tpb, 0)

    def out_tile(n):
        m = jnp.maximum(n - 1, 0)
        return (m // tpb, m % tpb, 0)

    def out_tile_t(n):
        m = jnp.maximum(n - 1, 0)
        return (m // tpb, 0, m % tpb)

    tile = pl.BlockSpec((1, TM, SEC), out_tile)
    tile_p = pl.BlockSpec((1, TM // 2, SEC), out_tile)
    tile_tp = pl.BlockSpec((1, SEC // 2, TM), out_tile_t)
    sds = jax.ShapeDtypeStruct((bsz, seq, SEC), BF16)
    sds_p = jax.ShapeDtypeStruct((bsz, seq // 2, SEC), jnp.uint32)
    sds_tp = jax.ShapeDtypeStruct((bsz, SEC // 2, seq), jnp.uint32)
    return pl.pallas_call(
        functools.partial(_inproj_kernel, tpb),
        grid=(n_tiles + 1,),
        in_specs=[
            pl.BlockSpec((1, TM, D_MODEL), in_tile),
            pl.BlockSpec((1, D_MODEL), const2),
            pl.BlockSpec((D_MODEL // 2, d_in), const2, pipeline_mode=pl.Buffered(1)),
            pl.BlockSpec((CONV_WIDTH, LRU_WIDTH), const2),
            pl.BlockSpec((1, LRU_WIDTH), const2),
            pl.BlockSpec((LRU_BLOCKS, LRU_BLOCK, 2 * LRU_BLOCK), const3),
            pl.BlockSpec((LRU_BLOCKS, 1, 2 * LRU_BLOCK), const3),
            pl.BlockSpec((1, LRU_WIDTH), const2),
            pl.BlockSpec((1, SEC), const2),
            pl.BlockSpec((1, SEC), const2),
            pl.BlockSpec((256, 256), const2),
        ],
        out_specs=[tile, tile_tp, tile_p, tile_tp, tile],
        out_shape=[sds, sds_tp, sds_p, sds_tp, sds],
        scratch_shapes=[
            pltpu.VMEM((TM + SUBLANES, LRU_WIDTH), F32),
            pltpu.VMEM((TM, LRU_WIDTH), F32),
            pltpu.VMEM((TM, LRU_WIDTH), F32),
            pltpu.VMEM((1, LRU_WIDTH), F32),
            pltpu.VMEM((TM, d_in), F32),
        ],
        compiler_params=pltpu.CompilerParams(
            dimension_semantics=("arbitrary",),
            vmem_limit_bytes=VMEM_LIMIT),
        name="inproj_lru_qknorm",
    )(x, ng, w_in, cw, cb, wg, bg, lam, gq, gk, gsum)


def _bucket_table():
    rel = np.arange(TA)[None, :] - np.arange(2 * TA)[:, None] + TA
    n = np.maximum(rel, 0)
    max_exact = N_BUCKETS // 2
    nf = np.maximum(n, 1).astype(np.float32)
    large = max_exact + (np.log(nf / np.float32(max_exact)) / np.float32(math.log(MAX_DISTANCE / max_exact))
                         * np.float32(N_BUCKETS - max_exact)).astype(np.int32)
    large = np.minimum(large, N_BUCKETS - 1)
    bucket = np.where(n < max_exact, n, large)
    return np.where(rel >= 0, bucket, -1).astype(np.int32)


def _attn_kernel(rb_ref, bucket_ref, lq1_ref, lk1_ref, lq2_ref, lk2_ref, sgain_ref,
                 qt_ref, qn_ref, k_ref, vt_ref, sg_ref, o_ref,
                 w_ref, m_ref, acc_ref, sa_ref, sb_ref, ma_ref, mb_ref):
    b = pl.program_id(0)
    i = pl.program_id(1)

    @pl.when((b == 0) & (i == 0))
    def _():
        bk = bucket_ref[...]
        for h in range(ATT_HEADS):
            far = rb_ref[N_BUCKETS - 1, h]
            w = jnp.full(bk.shape, NEG_INF, F32)
            for kk in range(N_BUCKETS):
                w = jnp.where(bk == kk, (rb_ref[kk, h] - far) * LOG2E, w)
            w_ref[h] = w

    lam = (jnp.exp(jnp.sum(lq1_ref[...] * lk1_ref[...], axis=-1, keepdims=True))
           - jnp.exp(jnp.sum(lq2_ref[...] * lk2_ref[...], axis=-1, keepdims=True))
           + LAM_INIT)

    zeros_half = jnp.zeros((HEAD_DIM, TA), BF16)

    def query_operand(q_packed):
        qh = pltpu.bitcast(q_packed, BF16)
        return jnp.concatenate(
            [jnp.concatenate([qh[0:HEAD_DIM, :], zeros_half], axis=0),
             jnp.concatenate([zeros_half, qh[HEAD_DIM:V_DIM, :]], axis=0)], axis=1)

    head_rows = [slice(h * HEAD_DIM, (h + 1) * HEAD_DIM) for h in range(ATT_HEADS)]
    qq = [query_operand(qt_ref[0, head_rows[h], :]) for h in range(ATT_HEADS)]

    ones_rows = (lax.broadcasted_iota(jnp.int32, (SUM_ROWS, TA), 0) == 0).astype(BF16)

    m_ref[...] = jnp.full(m_ref.shape, NEG_INF, F32)
    acc_ref[...] = jnp.zeros(acc_ref.shape, F32)

    last_tile = 2 * k_ref.shape[1] // TA - 1

    def scores_head(h, t, buf, cmax, q_operand=None):
        k0 = pl.multiple_of(jnp.minimum(t, last_tile) * (TA // 2), TA // 2)
        kt = pltpu.bitcast(k_ref[0, pl.ds(k0, TA // 2), h * V_DIM:(h + 1) * V_DIM],
                           BF16)
        qop = qq[h] if q_operand is None else q_operand
        s = jnp.dot(kt, qop, preferred_element_type=F32)
        buf[h] = s
        cmax[h] = jnp.max(s, axis=0, keepdims=True)

    def add_bias(t, buf, cmax):
        @pl.when(t == i - 1)
        def _():
            hq = TA // 2
            for h in range(ATT_HEADS):
                w = w_ref[h, hq:TA, 0:hq]
                for c0 in (0, TA):
                    lo = buf[h, hq:TA, c0:c0 + hq] + w
                    buf[h, hq:TA, c0:c0 + hq] = lo
                    top = jnp.max(buf[h, 0:hq, c0:c0 + hq], axis=0, keepdims=True)
                    cmax[h, :, c0:c0 + hq] = jnp.maximum(
                        top, jnp.max(lo, axis=0, keepdims=True))

    def softmax_pv_head(h, t, buf, cmax, diag=False):
        k0 = pl.multiple_of(t * TA, TA)
        s = buf[h]
        if diag:
            w = w_ref[h, TA:2 * TA, :]
            s = s + jnp.concatenate([w, w], axis=1)
            col_max = jnp.max(s, axis=0, keepdims=True)
        else:
            col_max = cmax[h]
        m_prev = m_ref[h]
        m_new = jnp.maximum(m_prev, col_max)
        alpha = jnp.exp2(m_prev - m_new)
        p = jnp.exp2(s - m_new).astype(BF16)
        vt = pltpu.bitcast(vt_ref[0, h * HEAD_DIM:(h + 1) * HEAD_DIM, pl.ds(k0, TA)],
                           BF16)
        vt = jnp.concatenate([vt, ones_rows], axis=0)
        acc_ref[h] = alpha * acc_ref[h] + jnp.dot(vt, p, preferred_element_type=F32)
        m_ref[h] = m_new

    def stage(t_cur, cur, t_next, nxt):
        for h in range(ATT_HEADS):
            scores_head(h, t_next, *nxt)
            softmax_pv_head(h, t_cur, *cur)
        add_bias(t_next, *nxt)

    buf_a = (sa_ref, ma_ref)
    buf_b = (sb_ref, mb_ref)
    n_tiles = i + 1

    @pl.when(i == 0)
    def _():
        for h in range(ATT_HEADS):
            scores_head(h, 0, *buf_a)

    add_bias(0, *buf_a)

    def pair_body(u, c):
        t0 = 2 * u
        stage(t0, buf_a, t0 + 1, buf_b)
        stage(t0 + 1, buf_b, t0 + 2, buf_a)
        return c

    n_pairs = n_tiles // 2
    lax.fori_loop(0, n_pairs - 1, pair_body, 0)

    @pl.when(n_pairs >= 1)
    def _():
        t0 = 2 * (n_pairs - 1)
        stage(t0, buf_a, t0 + 1, buf_b)

        @pl.when(n_tiles % 2 == 0)
        def _():
            for h in range(ATT_HEADS):
                scores_head(h, 0, *buf_a,
                            q_operand=query_operand(qn_ref[0, head_rows[h], :]))
                softmax_pv_head(h, t0 + 1, *buf_b, diag=True)

        @pl.when(n_tiles % 2 == 1)
        def _():
            for h in range(ATT_HEADS):
                scores_head(h, t0 + 2, *buf_a)
                softmax_pv_head(h, t0 + 1, *buf_b)

    @pl.when(n_tiles % 2 == 1)
    def _():
        for h in range(ATT_HEADS):
            softmax_pv_head(h, n_tiles - 1, *buf_a, diag=True)
            scores_head(h, 0, *buf_a, q_operand=query_operand(qn_ref[0, head_rows[h], :]))

    gain = sgain_ref[...] * (1.0 - LAM_INIT)
    for h in range(ATT_HEADS):
        hs = slice(h * V_DIM, (h + 1) * V_DIM)
        o = acc_ref[h, 0:V_DIM, :] * (1.0 / acc_ref[h, V_DIM:V_DIM + 1, :])
        o = o[:, 0:TA] - lam * o[:, TA:2 * TA]
        ms = jnp.mean(o * o, axis=0, keepdims=True)
        y = (o * lax.rsqrt(ms + EPS)).T * gain
        o_ref[0, :, hs] = (y * sg_ref[0, :, hs].astype(F32)).astype(BF16)


def _attention(rel_bias, lq1, lk1, lq2, lk2, sgain, qt, k, vt, sg):
    bsz, seq, width = sg.shape
    table = _bucket_table()
    near = table[:TA] != N_BUCKETS - 1
    assert not near[:TA // 2].any() and not near[:, TA // 2:].any()
    bucket = jnp.asarray(table)
    const2 = lambda b, i: (0, 0)
    return pl.pallas_call(
        _attn_kernel,
        grid=(bsz, seq // TA),
        in_specs=[
            pl.BlockSpec(memory_space=pltpu.SMEM),
            pl.BlockSpec((2 * TA, TA), const2),
            pl.BlockSpec((1, HEAD_DIM), const2),
            pl.BlockSpec((1, HEAD_DIM), const2),
            pl.BlockSpec((1, HEAD_DIM), const2),
            pl.BlockSpec((1, HEAD_DIM), const2),
            pl.BlockSpec((1, V_DIM), const2),
            pl.BlockSpec((1, width // 2, TA), lambda b, i: (b, 0, i)),
            pl.BlockSpec((1, width // 2, TA),
                         lambda b, i: (b, 0, jnp.minimum(i + 1, seq // TA - 1))),
            pl.BlockSpec((1, seq // 2, width), lambda b, i: (b, 0, 0)),
            pl.BlockSpec((1, width // 2, seq), lambda b, i: (b, 0, 0)),
            pl.BlockSpec((1, TA, width), lambda b, i: (b, i, 0)),
        ],
        out_specs=pl.BlockSpec((1, TA, width), lambda b, i: (b, i, 0)),
        out_shape=jax.ShapeDtypeStruct((bsz, seq, width), BF16),
        scratch_shapes=[
            pltpu.VMEM((ATT_HEADS, 2 * TA, TA), F32),
            pltpu.VMEM((ATT_HEADS, 1, 2 * TA), F32),
            pltpu.VMEM((ATT_HEADS, V_DIM + SUM_ROWS, 2 * TA), F32),
            pltpu.VMEM((ATT_HEADS, TA, 2 * TA), F32),
            pltpu.VMEM((ATT_HEADS, TA, 2 * TA), F32),
            pltpu.VMEM((ATT_HEADS, 1, 2 * TA), F32),
            pltpu.VMEM((ATT_HEADS, 1, 2 * TA), F32),
        ],
        compiler_params=pltpu.CompilerParams(
            dimension_semantics=("arbitrary", "arbitrary"),
            vmem_limit_bytes=VMEM_LIMIT),
        name="diff_attention",
    )(rel_bias, bucket, lq1, lk1, lq2, lk2, sgain, qt, qt, k, vt, sg)


def _outproj_kernel(x_ref, ya_ref, yb_ref, w_ref, o_ref):
    acc = jnp.dot(ya_ref[...], w_ref[0:LRU_WIDTH, :], preferred_element_type=F32)
    acc = acc + jnp.dot(yb_ref[...], w_ref[LRU_WIDTH:, :], preferred_element_type=F32)
    o_ref[...] = x_ref[...] + acc


def _outproj(x2, ya, yb, w_out):
    n, d = x2.shape
    row = lambda i: (i, 0)
    return pl.pallas_call(
        _outproj_kernel,
        grid=(n // TM_OUT,),
        in_specs=[
            pl.BlockSpec((TM_OUT, d), row),
            pl.BlockSpec((TM_OUT, ya.shape[1]), row),
            pl.BlockSpec((TM_OUT, yb.shape[1]), row),
            pl.BlockSpec(w_out.shape, lambda i: (0, 0)),
        ],
        out_specs=pl.BlockSpec((TM_OUT, d), row),
        out_shape=jax.ShapeDtypeStruct((n, d), F32),
        compiler_params=pltpu.CompilerParams(
            dimension_semantics=("arbitrary",), vmem_limit_bytes=VMEM_LIMIT),
        name="outproj_residual",
    )(x2, ya, yb, w_out)


def _pack_rows(w):
    rows, cols = w.shape

    def pack_kernel(w_ref, o_ref):
        o_ref[...] = _packed(w_ref[...])

    return pl.pallas_call(
        pack_kernel,
        grid=(cols // PACK_COLS,),
        in_specs=[pl.BlockSpec((rows, PACK_COLS), lambda j: (0, j))],
        out_specs=pl.BlockSpec((rows // 2, PACK_COLS), lambda j: (0, j)),
        out_shape=jax.ShapeDtypeStruct((rows // 2, cols), jnp.uint32),
        name="pack_weight_rows",
    )(w)


def kernel(x, norm_gain, w_in, conv_w, conv_b, w_rg, b_rg, w_ig, b_ig, lru_lambda,
           q_norm_gain, k_norm_gain, lambda_q1, lambda_k1, lambda_q2, lambda_k2,
           subln_gain, w_out, rel_bias):
    bsz, seq, d = x.shape
    assert norm_gain.shape[0] == 1, "single layer only"
    assert seq % TM == 0 and seq % TA == 0 and (bsz * seq) % TM_OUT == 0
    n_groups = SEC // HEAD_DIM
    wg = (0.5 * jnp.concatenate([w_rg[0], w_ig[0]], axis=-1)).astype(BF16)
    bg = 0.5 * jnp.concatenate([b_rg[0], b_ig[0]], axis=-1)[:, None, :]
    gq = jnp.tile(q_norm_gain[0], n_groups)[None, :]
    gk = jnp.tile(k_norm_gain[0], n_groups)[None, :]
    gsum = jnp.asarray(np.kron(np.eye(256 // HEAD_DIM), np.ones((HEAD_DIM, HEAD_DIM))) / HEAD_DIM,
                       BF16)

    ylru, qt, k, vt, sg = _inproj(
        x, norm_gain, _pack_rows(w_in[0]), conv_w[0], conv_b, wg, bg, lru_lambda,
        gq, gk, gsum)
    yatt = _attention(rel_bias, lambda_q1, lambda_k1, lambda_q2, lambda_k2, subln_gain,
                      qt, k, vt, sg)
    out = _outproj(x.reshape(bsz * seq, d), ylru.reshape(bsz * seq, -1),
                   yatt.reshape(bsz * seq, -1), w_out[0].astype(BF16))
    return out.reshape(bsz, seq, d)
```

```python
import functools
import math

import jax
import jax.numpy as jnp
import numpy as np
from jax import lax
from jax.experimental import pallas as pl
from jax.experimental.pallas import tpu as pltpu

F32 = jnp.float32
BF16 = jnp.bfloat16

D_MODEL = 1024
LRU_WIDTH = 1024
LRU_BLOCKS = 8
LRU_BLOCK = LRU_WIDTH // LRU_BLOCKS
CONV_WIDTH = 4
LRU_C = 8.0
ATT_HEADS = 8
HEAD_DIM = 64
V_DIM = 2 * HEAD_DIM
SEC = 1024
N_BUCKETS = 32
MAX_DISTANCE = 128
EPS = 1e-6
NEG_INF = -1e30
TINY = 1e-30
LAM_INIT = 0.8 - 0.6 * math.exp(-0.3 * 0)
LOG2E = math.log2(math.e)

SUBLANES = 8
TM = 512
TM_OUT = 1024
SLAB = 256
PACK_COLS = 512
TA = 256
SUM_ROWS = 16
VMEM_LIMIT = 58 * 1024 * 1024


def _silu(x):
    hx = 0.5 * x
    return hx + hx * jnp.tanh(hx)


def _packed(x):
    return pltpu.bitcast(x.astype(BF16), jnp.uint32)


def _inproj_kernel(tiles_per_batch,
                   x_ref, ng_ref, w_ref, cw_ref, cb_ref, wg_ref, bg_ref, lam_ref,
                   gq_ref, gk_ref, gsum_ref,
                   ylru_ref, qt_ref, k_ref, vt_ref, sg_ref,
                   xbuf, a_buf, u_buf, hcar, zbuf):
    step = pl.program_id(0)

    @pl.when(step == 0)
    def _():
        zbuf[...] = jnp.zeros_like(zbuf)

    @pl.when(jnp.maximum(step - 1, 0) % tiles_per_batch == 0)
    def _():
        xbuf[0:SUBLANES, :] = jnp.zeros((SUBLANES, LRU_WIDTH), F32)
        hcar[...] = jnp.zeros_like(hcar)

    x = x_ref[0]
    ms = jnp.mean(x * x, axis=-1, keepdims=True)
    hb = (x * lax.rsqrt(ms + EPS) * ng_ref[...]).astype(BF16)

    def proj(c, s):
        col = c * SEC + s * SLAB
        prev = zbuf[:, col:col + SLAB]
        w_slab = pltpu.bitcast(w_ref[:, col:col + SLAB], BF16)
        zbuf[:, col:col + SLAB] = jnp.dot(hb, w_slab, preferred_element_type=F32)
        return prev

    def qk_norm(zc, gain):
        sq = (zc * zc).astype(BF16)
        mean_sq = jnp.dot(sq, gsum_ref[...], preferred_element_type=F32)
        return zc * lax.rsqrt(mean_sq + EPS) * gain

    z = -lam_ref[...]
    softplus = jnp.maximum(z, 0.0) + jnp.log1p(jnp.exp(-jnp.abs(z)))
    half_log2_a_max = (-0.5 * LRU_C * LOG2E) * softplus
    gq = gq_ref[...] * (LOG2E / math.sqrt(HEAD_DIM))
    row = lax.broadcasted_iota(jnp.int32, (SUBLANES, SLAB), 0)
    n_slabs = LRU_WIDTH // SLAB

    xl_next = proj(0, 0)
    for s in range(n_slabs):
        ss = slice(s * SLAB, (s + 1) * SLAB)
        xl = xl_next
        if s + 1 < n_slabs:
            xl_next = proj(0, s + 1)

        xbuf[SUBLANES:SUBLANES + TM, ss] = xl
        xc = cw_ref[3:4, ss] * xl + cb_ref[:, ss]
        for d in range(1, CONV_WIDTH):
            xc = xc + cw_ref[3 - d:4 - d, ss] * xbuf[SUBLANES - d:SUBLANES - d + TM, ss]
        xbuf[0:SUBLANES, ss] = xbuf[TM:TM + SUBLANES, ss]

        xcb = xc.astype(BF16)
        for j in range(SLAB // LRU_BLOCK):
            blk = s * (SLAB // LRU_BLOCK) + j
            cs = slice(blk * LRU_BLOCK, (blk + 1) * LRU_BLOCK)
            js = slice(j * LRU_BLOCK, (j + 1) * LRU_BLOCK)
            th = jnp.tanh(jnp.dot(xcb[:, js], wg_ref[blk], preferred_element_type=F32)
                          + bg_ref[blk])
            c2 = half_log2_a_max[:, cs]
            a = jnp.exp2(c2 * th[:, :LRU_BLOCK] + c2)
            hxc = 0.5 * xc[:, js]
            gated_x = hxc * th[:, LRU_BLOCK:] + hxc
            y = 1.0 - a * a
            a_buf[:, cs] = a
            u_buf[:, cs] = (y * lax.rsqrt(jnp.maximum(y, TINY))) * gated_x
            if j == 0:
                zq = proj(2, s)

        carry = hcar[:, ss]
        n_groups = TM // SUBLANES
        for g in range(n_groups):
            if g == 0:
                zk = proj(3, s)
            elif g == n_groups // 3:
                gl = proj(1, s)
            elif g == 2 * n_groups // 3:
                zv = proj(4, s)
            rs = slice(g * SUBLANES, (g + 1) * SUBLANES)
            a = a_buf[rs, ss]
            u = u_buf[rs, ss]
            for k in (1, 2, 4):
                keep = row >= k
                a_sh = jnp.where(keep, pltpu.roll(a, k, 0), 1.0)
                u_sh = jnp.where(keep, pltpu.roll(u, k, 0), 0.0)
                u = a * u_sh + u
                a = a * a_sh
            h = a * carry + u
            u_buf[rs, ss] = h
            carry = h[SUBLANES - 1:SUBLANES, :]
        hcar[:, ss] = carry

        zg = proj(5, s)
        qt_ref[0, s * SLAB // 2:(s + 1) * SLAB // 2, :] = _packed(qk_norm(zq, gq[:, ss]).T)
        k_ref[0, :, ss] = _packed(qk_norm(zk, gk_ref[:, ss]))
        ylru_ref[0, :, ss] = (u_buf[:, ss] * _silu(gl)).astype(BF16)
        a_buf[:, ss] = zv
        vt_ref[0, s * SLAB // 2:(s + 1) * SLAB // 2, :] = _packed(a_buf[:, ss].T)
        sg_ref[0, :, ss] = _silu(zg).astype(BF16)


def _inproj(x, ng, w_in, cw, cb, wg, bg, lam, gq, gk, gsum):
    bsz, seq, _ = x.shape
    d_in = w_in.shape[1]
    tpb = seq // TM
    n_tiles = bsz * tpb
    const2 = lambda n: (0, 0)
    const3 = lambda n: (0, 0, 0)

    def in_tile(n):
        m = jnp.minimum(n, n_tiles - 1)
        return (m // tpb, m % tpb, 0)

    def out_tile(n):
        m = jnp.maximum(n - 1, 0)
        return (m // tpb, m % tpb, 0)

    def out_tile_t(n):
        m = jnp.maximum(n - 1, 0)
        return (m // tpb, 0, m % tpb)

    tile = pl.BlockSpec((1, TM, SEC), out_tile)
    tile_p = pl.BlockSpec((1, TM // 2, SEC), out_tile)
    tile_tp = pl.BlockSpec((1, SEC // 2, TM), out_tile_t)
    sds = jax.ShapeDtypeStruct((bsz, seq, SEC), BF16)
    sds_p = jax.ShapeDtypeStruct((bsz, seq // 2, SEC), jnp.uint32)
    sds_tp = jax.ShapeDtypeStruct((bsz, SEC // 2, seq), jnp.uint32)
    return pl.pallas_call(
        functools.partial(_inproj_kernel, tpb),
        grid=(n_tiles + 1,),
        in_specs=[
            pl.BlockSpec((1, TM, D_MODEL), in_tile),
            pl.BlockSpec((1, D_MODEL), const2),
            pl.BlockSpec((D_MODEL // 2, d_in), const2, pipeline_mode=pl.Buffered(1)),
            pl.BlockSpec((CONV_WIDTH, LRU_WIDTH), const2),
            pl.BlockSpec((1, LRU_WIDTH), const2),
            pl.BlockSpec((LRU_BLOCKS, LRU_BLOCK, 2 * LRU_BLOCK), const3),
            pl.BlockSpec((LRU_BLOCKS, 1, 2 * LRU_BLOCK), const3),
            pl.BlockSpec((1, LRU_WIDTH), const2),
            pl.BlockSpec((1, SEC), const2),
            pl.BlockSpec((1, SEC), const2),
            pl.BlockSpec((256, 256), const2),
        ],
        out_specs=[tile, tile_tp, tile_p, tile_tp, tile],
        out_shape=[sds, sds_tp, sds_p, sds_tp, sds],
        scratch_shapes=[
            pltpu.VMEM((TM + SUBLANES, LRU_WIDTH), F32),
            pltpu.VMEM((TM, LRU_WIDTH), F32),
            pltpu.VMEM((TM, LRU_WIDTH), F32),
            pltpu.VMEM((1, LRU_WIDTH), F32),
            pltpu.VMEM((TM, d_in), F32),
        ],
        compiler_params=pltpu.CompilerParams(
            dimension_semantics=("arbitrary",),
            vmem_limit_bytes=VMEM_LIMIT),
        name="inproj_lru_qknorm",
    )(x, ng, w_in, cw, cb, wg, bg, lam, gq, gk, gsum)


def _bucket_table():
    rel = np.arange(TA)[None, :] - np.arange(2 * TA)[:, None] + TA
    n = np.maximum(rel, 0)
    max_exact = N_BUCKETS // 2
    nf = np.maximum(n, 1).astype(np.float32)
    large = max_exact + (np.log(nf / np.float32(max_exact)) / np.float32(math.log(MAX_DISTANCE / max_exact))
                         * np.float32(N_BUCKETS - max_exact)).astype(np.int32)
    large = np.minimum(large, N_BUCKETS - 1)
    bucket = np.where(n < max_exact, n, large)
    return np.where(rel >= 0, bucket, -1).astype(np.int32)


def _attn_kernel(rb_ref, bucket_ref, lq1_ref, lk1_ref, lq2_ref, lk2_ref, sgain_ref,
                 qt_ref, qn_ref, k_ref, vt_ref, sg_ref, o_ref,
                 w_ref, m_ref, acc_ref, sa_ref, sb_ref, ma_ref, mb_ref):
    b = pl.program_id(0)
    i = pl.program_id(1)

    @pl.when((b == 0) & (i == 0))
    def _():
        bk = bucket_ref[...]
        for h in range(ATT_HEADS):
            far = rb_ref[N_BUCKETS - 1, h]
            w = jnp.full(bk.shape, NEG_INF, F32)
            for kk in range(N_BUCKETS):
                w = jnp.where(bk == kk, (rb_ref[kk, h] - far) * LOG2E, w)
            w_ref[h] = w

    lam = (jnp.exp(jnp.sum(lq1_ref[...] * lk1_ref[...], axis=-1, keepdims=True))
           - jnp.exp(jnp.sum(lq2_ref[...] * lk2_ref[...], axis=-1, keepdims=True))
           + LAM_INIT)

    zeros_half = jnp.zeros((HEAD_DIM, TA), BF16)

    def query_operand(q_packed):
        qh = pltpu.bitcast(q_packed, BF16)
        return jnp.concatenate(
            [jnp.concatenate([qh[0:HEAD_DIM, :], zeros_half], axis=0),
             jnp.concatenate([zeros_half, qh[HEAD_DIM:V_DIM, :]], axis=0)], axis=1)

    head_rows = [slice(h * HEAD_DIM, (h + 1) * HEAD_DIM) for h in range(ATT_HEADS)]
    qq = [query_operand(qt_ref[0, head_rows[h], :]) for h in range(ATT_HEADS)]

    ones_rows = (lax.broadcasted_iota(jnp.int32, (SUM_ROWS, TA), 0) == 0).astype(BF16)

    m_ref[...] = jnp.full(m_ref.shape, NEG_INF, F32)
    acc_ref[...] = jnp.zeros(acc_ref.shape, F32)

    last_tile = 2 * k_ref.shape[1] // TA - 1

    def scores_head(h, t, buf, cmax, q_operand=None):
        k0 = pl.multiple_of(jnp.minimum(t, last_tile) * (TA // 2), TA // 2)
        kt = pltpu.bitcast(k_ref[0, pl.ds(k0, TA // 2), h * V_DIM:(h + 1) * V_DIM],
                           BF16)
        qop = qq[h] if q_operand is None else q_operand
        s = jnp.dot(kt, qop, preferred_element_type=F32)
        buf[h] = s
        cmax[h] = jnp.max(s, axis=0, keepdims=True)

    def add_bias(t, buf, cmax):
        @pl.when(t == i)
        def _():
            for h in range(ATT_HEADS):
                w = w_ref[h, TA:2 * TA, :]
                s = buf[h] + jnp.concatenate([w, w], axis=1)
                buf[h] = s
                cmax[h] = jnp.max(s, axis=0, keepdims=True)

        @pl.when(t == i - 1)
        def _():
            hq = TA // 2
            for h in range(ATT_HEADS):
                w = w_ref[h, hq:TA, 0:hq]
                for c0 in (0, TA):
                    lo = buf[h, hq:TA, c0:c0 + hq] + w
                    buf[h, hq:TA, c0:c0 + hq] = lo
                    top = jnp.max(buf[h, 0:hq, c0:c0 + hq], axis=0, keepdims=True)
                    cmax[h, :, c0:c0 + hq] = jnp.maximum(
                        top, jnp.max(lo, axis=0, keepdims=True))

    def softmax_pv_head(h, t, buf, cmax):
        k0 = pl.multiple_of(t * TA, TA)
        s = buf[h]
        m_prev = m_ref[h]
        m_new = jnp.maximum(m_prev, cmax[h])
        alpha = jnp.exp2(m_prev - m_new)
        p = jnp.exp2(s - m_new).astype(BF16)
        vt = pltpu.bitcast(vt_ref[0, h * HEAD_DIM:(h + 1) * HEAD_DIM, pl.ds(k0, TA)],
                           BF16)
        vt = jnp.concatenate([vt, ones_rows], axis=0)
        acc_ref[h] = alpha * acc_ref[h] + jnp.dot(vt, p, preferred_element_type=F32)
        m_ref[h] = m_new

    def stage(t_cur, cur, t_next, nxt):
        for h in range(ATT_HEADS):
            scores_head(h, t_next, *nxt)
            softmax_pv_head(h, t_cur, *cur)
        add_bias(t_next, *nxt)

    buf_a = (sa_ref, ma_ref)
    buf_b = (sb_ref, mb_ref)
    n_tiles = i + 1

    @pl.when(i == 0)
    def _():
        for h in range(ATT_HEADS):
            scores_head(h, 0, *buf_a)

    add_bias(0, *buf_a)

    def pair_body(u, c):
        t0 = 2 * u
        stage(t0, buf_a, t0 + 1, buf_b)
        stage(t0 + 1, buf_b, t0 + 2, buf_a)
        return c

    n_pairs = n_tiles // 2
    lax.fori_loop(0, n_pairs - 1, pair_body, 0)

    @pl.when(n_pairs >= 1)
    def _():
        t0 = 2 * (n_pairs - 1)
        stage(t0, buf_a, t0 + 1, buf_b)

        @pl.when(n_tiles % 2 == 0)
        def _():
            for h in range(ATT_HEADS):
                softmax_pv_head(h, t0 + 1, *buf_b)

        @pl.when(n_tiles % 2 == 1)
        def _():
            stage(t0 + 1, buf_b, t0 + 2, buf_a)

    @pl.when(n_tiles % 2 == 1)
    def _():
        for h in range(ATT_HEADS):
            softmax_pv_head(h, n_tiles - 1, *buf_a)

    gain = sgain_ref[...] * (1.0 - LAM_INIT)
    for h in range(ATT_HEADS):
        hs = slice(h * V_DIM, (h + 1) * V_DIM)
        for c0 in range(0, TA, TA // 2):
            c1 = slice(c0, c0 + TA // 2)
            c2 = slice(TA + c0, TA + c0 + TA // 2)
            r1 = 1.0 / acc_ref[h, V_DIM:V_DIM + 1, c1]
            r2 = lam / acc_ref[h, V_DIM:V_DIM + 1, c2]
            o = acc_ref[h, 0:V_DIM, c1] * r1 - acc_ref[h, 0:V_DIM, c2] * r2
            ms = jnp.mean(o * o, axis=0, keepdims=True)
            y = (o * lax.rsqrt(ms + EPS)).T * gain
            o_ref[0, c1, hs] = (y * sg_ref[0, c1, hs].astype(F32)).astype(BF16)
        scores_head(h, 0, *buf_a, q_operand=query_operand(qn_ref[0, head_rows[h], :]))


def _attention(rel_bias, lq1, lk1, lq2, lk2, sgain, qt, k, vt, sg):
    bsz, seq, width = sg.shape
    table = _bucket_table()
    near = table[:TA] != N_BUCKETS - 1
    assert not near[:TA // 2].any() and not near[:, TA // 2:].any()
    bucket = jnp.asarray(table)
    const2 = lambda b, i: (0, 0)
    return pl.pallas_call(
        _attn_kernel,
        grid=(bsz, seq // TA),
        in_specs=[
            pl.BlockSpec(memory_space=pltpu.SMEM),
            pl.BlockSpec((2 * TA, TA), const2),
            pl.BlockSpec((1, HEAD_DIM), const2),
            pl.BlockSpec((1, HEAD_DIM), const2),
            pl.BlockSpec((1, HEAD_DIM), const2),
            pl.BlockSpec((1, HEAD_DIM), const2),
            pl.BlockSpec((1, V_DIM), const2),
            pl.BlockSpec((1, width // 2, TA), lambda b, i: (b, 0, i)),
            pl.BlockSpec((1, width // 2, TA),
                         lambda b, i: (b, 0, jnp.minimum(i + 1, seq // TA - 1))),
            pl.BlockSpec((1, seq // 2, width), lambda b, i: (b, 0, 0)),
            pl.BlockSpec((1, width // 2, seq), lambda b, i: (b, 0, 0)),
            pl.BlockSpec((1, TA, width), lambda b, i: (b, i, 0)),
        ],
        out_specs=pl.BlockSpec((1, TA, width), lambda b, i: (b, i, 0)),
        out_shape=jax.ShapeDtypeStruct((bsz, seq, width), BF16),
        scratch_shapes=[
            pltpu.VMEM((ATT_HEADS, 2 * TA, TA), F32),
            pltpu.VMEM((ATT_HEADS, 1, 2 * TA), F32),
            pltpu.VMEM((ATT_HEADS, V_DIM + SUM_ROWS, 2 * TA), F32),
            pltpu.VMEM((ATT_HEADS, TA, 2 * TA), F32),
            pltpu.VMEM((ATT_HEADS, TA, 2 * TA), F32),
            pltpu.VMEM((ATT_HEADS, 1, 2 * TA), F32),
            pltpu.VMEM((ATT_HEADS, 1, 2 * TA), F32),
        ],
        compiler_params=pltpu.CompilerParams(
            dimension_semantics=("arbitrary", "arbitrary"),
            vmem_limit_bytes=VMEM_LIMIT),
        name="diff_attention",
    )(rel_bias, bucket, lq1, lk1, lq2, lk2, sgain, qt, qt, k, vt, sg)


def _outproj_kernel(x_ref, ya_ref, yb_ref, w_ref, o_ref):
    acc = jnp.dot(ya_ref[...], w_ref[0:LRU_WIDTH, :], preferred_element_type=F32)
    acc = acc + jnp.dot(yb_ref[...], w_ref[LRU_WIDTH:, :], preferred_element_type=F32)
    o_ref[...] = x_ref[...] + acc


def _outproj(x2, ya, yb, w_out):
    n, d = x2.shape
    row = lambda i: (i, 0)
    return pl.pallas_call(
        _outproj_kernel,
        grid=(n // TM_OUT,),
        in_specs=[
            pl.BlockSpec((TM_OUT, d), row),
            pl.BlockSpec((TM_OUT, ya.shape[1]), row),
            pl.BlockSpec((TM_OUT, yb.shape[1]), row),
            pl.BlockSpec(w_out.shape, lambda i: (0, 0)),
        ],
        out_specs=pl.BlockSpec((TM_OUT, d), row),
        out_shape=jax.ShapeDtypeStruct((n, d), F32),
        compiler_params=pltpu.CompilerParams(
            dimension_semantics=("arbitrary",), vmem_limit_bytes=VMEM_LIMIT),
        name="outproj_residual",
    )(x2, ya, yb, w_out)


def _pack_rows(w):
    rows, cols = w.shape

    def pack_kernel(w_ref, o_ref):
        o_ref[...] = _packed(w_ref[...])

    return pl.pallas_call(
        pack_kernel,
        grid=(cols // PACK_COLS,),
        in_specs=[pl.BlockSpec((rows, PACK_COLS), lambda j: (0, j))],
        out_specs=pl.BlockSpec((rows // 2, PACK_COLS), lambda j: (0, j)),
        out_shape=jax.ShapeDtypeStruct((rows // 2, cols), jnp.uint32),
        name="pack_weight_rows",
    )(w)


def kernel(x, norm_gain, w_in, conv_w, conv_b, w_rg, b_rg, w_ig, b_ig, lru_lambda,
           q_norm_gain, k_norm_gain, lambda_q1, lambda_k1, lambda_q2, lambda_k2,
           subln_gain, w_out, rel_bias):
    bsz, seq, d = x.shape
    assert norm_gain.shape[0] == 1, "single layer only"
    assert seq % TM == 0 and seq % TA == 0 and (bsz * seq) % TM_OUT == 0
    n_groups = SEC // HEAD_DIM
    wg = (0.5 * jnp.concatenate([w_rg[0], w_ig[0]], axis=-1)).astype(BF16)
    bg = 0.5 * jnp.concatenate([b_rg[0], b_ig[0]], axis=-1)[:, None, :]
    gq = jnp.tile(q_norm_gain[0], n_groups)[None, :]
    gk = jnp.tile(k_norm_gain[0], n_groups)[None, :]
    gsum = jnp.asarray(np.kron(np.eye(256 // HEAD_DIM), np.ones((HEAD_DIM, HEAD_DIM))) / HEAD_DIM,
                       BF16)

    ylru, qt, k, vt, sg = _inproj(
        x, norm_gain, _pack_rows(w_in[0]), conv_w[0], conv_b, wg, bg, lru_lambda,
        gq, gk, gsum)
    yatt = _attention(rel_bias, lambda_q1, lambda_k1, lambda_q2, lambda_k2, subln_gain,
                      qt, k, vt, sg)
    out = _outproj(x.reshape(bsz * seq, d), ylru.reshape(bsz * seq, -1),
                   yatt.reshape(bsz * seq, -1), w_out[0].astype(BF16))
    return out.reshape(bsz, seq, d)
```
